```python
import math
import jax, jax.numpy as jnp
from jax import lax
import numpy as np

D_MODEL = 1024
BATCH = 8
SEQ = 2048
DEPTH = 1
DEC_BATCH = 4
DEC_SEQ = 4096
PAST_LEN = 128

D_RNN = D_MODEL
RNN_HEADS = 16
RNN_HEAD_DIM = D_RNN // RNN_HEADS
RNN_CONV = 4
RNN_CONV_PAD = (2, 1)
RG_C = 8.0
A_MIN = 0.9
A_MAX = 0.999
D_CONV = D_MODEL
CONV_WIDTH = 31
CONV_PAD = (CONV_WIDTH // 2, CONV_WIDTH // 2)
N_EXPERTS = 32
TOP_K = 4
D_EXPERT = D_MODEL
SWIGLU_LIMIT = 7.0
SWIGLU_ALPHA = 1.702
MOE_BLOCK = 128
D_PLE = 256
EPS = 1e-6
SPLITS = [D_RNN, 2 * D_RNN, 2 * D_RNN + D_CONV, 2 * D_RNN + 2 * D_CONV, 2 * D_RNN + 2 * D_CONV + D_MODEL]
D_IN = 2 * D_RNN + 2 * D_CONV + 2 * D_MODEL

kernel_name = "hybrid_rglru_conformer_moe_encoder"


def rmsnorm(x, g):
    xf = x.astype(jnp.float32)
    y = xf * lax.rsqrt(jnp.mean(xf * xf, axis=-1, keepdims=True) + EPS)
    return (y * g.astype(jnp.float32)).astype(x.dtype)


def layernorm(x, g, b):
    xf = x.astype(jnp.float32)
    mu = jnp.mean(xf, axis=-1, keepdims=True)
    var = jnp.mean(jnp.square(xf - mu), axis=-1, keepdims=True)
    y = (xf - mu) * lax.rsqrt(var + EPS)
    return (y * g.astype(jnp.float32) + b.astype(jnp.float32)).astype(x.dtype)


def depthwise_conv(x, w, b, pad):
    c = x.shape[-1]
    y = lax.conv_general_dilated(x, w[:, None, :].astype(x.dtype), window_strides=(1,), padding=[pad],
                                 dimension_numbers=("NWC", "WIO", "NWC"), feature_group_count=c)
    return y + b


def _lin_comb(c1, c2):
    a1, b1 = c1
    a2, b2 = c2
    return a1 * a2, a2 * b1 + b2


def rglru_direction(xc, w_a, b_a, w_x, b_x, lam, reverse):
    bsz, s, _ = xc.shape
    xh = xc.reshape(bsz, s, RNN_HEADS, RNN_HEAD_DIM)
    r = jax.nn.sigmoid(jnp.einsum("bshi,hij->bshj", xh, w_a).reshape(bsz, s, D_RNN) + b_a)
    i = jax.nn.sigmoid(jnp.einsum("bshi,hij->bshj", xh, w_x).reshape(bsz, s, D_RNN) + b_x)
    log_a = -RG_C * r.astype(jnp.float32) * jax.nn.softplus(-lam.astype(jnp.float32))
    a = jnp.exp(log_a)
    mult = jnp.sqrt(-jnp.expm1(2.0 * log_a))
    u = mult * (i * xc).astype(jnp.float32)
    _, h = lax.associative_scan(_lin_comb, (a, u), reverse=reverse, axis=1)
    return h


def mixer(h, w_in, b_in, conv_rnn_w, conv_rnn_b, w_a, b_a, w_x, b_x, lam, w_rnn_proj,
          conv_dw_w, conv_dw_b, ln_g, ln_b, w_conv_proj, b_conv_proj, w_out):
    z = h @ w_in + b_in
    xr, yr, cv, cg, g_rnn, g_conv = jnp.split(z, SPLITS, axis=-1)
    xc = depthwise_conv(xr, conv_rnn_w, conv_rnn_b, RNN_CONV_PAD)
    hf = rglru_direction(xc, w_a[0], b_a[0], w_x[0], b_x[0], lam[0], reverse=False)
    hb = rglru_direction(xc, w_a[1], b_a[1], w_x[1], b_x[1], lam[1], reverse=True)
    rnn = ((hf + hb).astype(h.dtype) * jax.nn.gelu(yr)) @ w_rnn_proj
    u = cv * jax.nn.sigmoid(cg)
    u = depthwise_conv(u, conv_dw_w, conv_dw_b, CONV_PAD)
    u = jax.nn.silu(layernorm(u, ln_g, ln_b))
    conv = u @ w_conv_proj + b_conv_proj
    merged = jax.nn.sigmoid(g_rnn) * rnn + jax.nn.sigmoid(g_conv) * conv
    return merged @ w_out


def moe(h, w_router, b_router, w_gu, b_gu, w_down, b_down):
    t, d = h.shape
    tk = t * TOP_K
    logits = (h @ w_router + b_router).astype(jnp.float32)
    top_val, top_idx = lax.top_k(logits, TOP_K)
    gate = jax.nn.softmax(top_val, axis=-1)
    flat_e = top_idx.reshape(-1)
    flat_tok = jnp.arange(tk, dtype=jnp.int32) // TOP_K
    flat_gate = gate.reshape(-1)
    order = jnp.argsort(flat_e, stable=True)
    sorted_e = flat_e[order]
    counts = jnp.bincount(flat_e, length=N_EXPERTS)
    padded = (counts + MOE_BLOCK - 1) // MOE_BLOCK * MOE_BLOCK
    pad_end = jnp.cumsum(padded)
    pad_start = pad_end - padded
    start = jnp.cumsum(counts) - counts
    dest = pad_start[sorted_e] + jnp.arange(tk) - start[sorted_e]
    n_blocks = -(-tk // MOE_BLOCK) + N_EXPERTS
    n_rows = n_blocks * MOE_BLOCK
    row_tok = jnp.full((n_rows,), t, jnp.int32).at[dest].set(flat_tok[order])
    row_gate = jnp.zeros((n_rows,), jnp.float32).at[dest].set(flat_gate[order])
    block_e = jnp.minimum(jnp.searchsorted(pad_end, jnp.arange(n_blocks) * MOE_BLOCK, side="right"), N_EXPERTS - 1)
    h_pad = jnp.concatenate([h, jnp.zeros((1, d), h.dtype)], axis=0)

    def expert_block(args):
        tok, g, e = args
        xb = h_pad[tok]
        gu = xb @ w_gu[e] + b_gu[e]
        gt = jnp.minimum(gu[:, :D_EXPERT], SWIGLU_LIMIT)
        up = jnp.clip(gu[:, D_EXPERT:], -SWIGLU_LIMIT, SWIGLU_LIMIT)
        act = (up + 1.0) * (gt * jax.nn.sigmoid(gt * SWIGLU_ALPHA))
        yb = act @ w_down[e] + b_down[e]
        return yb * g[:, None].astype(yb.dtype)

    ys = lax.map(expert_block, (row_tok.reshape(n_blocks, MOE_BLOCK), row_gate.reshape(n_blocks, MOE_BLOCK), block_e))
    out = jnp.zeros((t + 1, d), ys.dtype).at[row_tok].add(ys.reshape(n_rows, d))
    return out[:t]


def trunk(x, p, weights):
    (norm_mix_g, w_in, b_in, conv_rnn_w, conv_rnn_b, w_a, b_a, w_x, b_x, lam, w_rnn_proj,
     conv_dw_w, conv_dw_b, ln_g, ln_b, w_conv_proj, b_conv_proj, w_out,
     norm_ffn_g, w_router, b_router, w_gu, b_gu, w_down, b_down,
     norm_ple_g, w_ple_gate, w_ple_proj, ple_post_g, final_g) = weights
    bsz, s, d = x.shape
    for i in range(DEPTH):
        h = rmsnorm(x, norm_mix_g[i])
        x = x + mixer(h, w_in[i], b_in[i], conv_rnn_w[i], conv_rnn_b[i], w_a[i], b_a[i], w_x[i], b_x[i],
                      lam[i], w_rnn_proj[i], conv_dw_w[i], conv_dw_b[i], ln_g[i], ln_b[i],
                      w_conv_proj[i], b_conv_proj[i], w_out[i])
        h = rmsnorm(x, norm_ffn_g[i]).reshape(bsz * s, d)
        x = x + moe(h, w_router[i], b_router[i], w_gu[i], b_gu[i], w_down[i], b_down[i]).reshape(bsz, s, d)
        h = rmsnorm(x, norm_ple_g[i])
        e = rmsnorm(p[i] @ w_ple_proj[i], ple_post_g[i])
        x = x + jax.nn.sigmoid(h @ w_ple_gate[i]) * e
    return rmsnorm(x, final_g)


def setup_inputs(seed: int = 0) -> dict:
    key = jax.random.key(seed)
    ks = iter(jax.random.split(key, 40))
    f32 = jnp.float32

    def nrm(shape, scale):
        return jax.random.normal(next(ks), shape, f32) * scale

    def gain(shape):
        return 1.0 + nrm(shape, 0.02)

    a0 = jax.random.uniform(next(ks), (DEPTH, 2, D_RNN), f32, A_MIN, A_MAX)
    return {
        "x_prompt": nrm((BATCH, SEQ, D_MODEL), 1.0),
        "x_sample": nrm((DEC_BATCH, DEC_SEQ, D_MODEL), 1.0),
        "p_prompt": nrm((DEPTH, BATCH, SEQ, D_PLE), 1.0),
        "p_sample": nrm((DEPTH, DEC_BATCH, DEC_SEQ, D_PLE), 1.0),
        "norm_mix_g": gain((DEPTH, D_MODEL)),
        "w_in": nrm((DEPTH, D_MODEL, D_IN), D_MODEL ** -0.5),
        "b_in": nrm((DEPTH, D_IN), 0.02),
        "conv_rnn_w": nrm((DEPTH, RNN_CONV, D_RNN), RNN_CONV ** -0.5),
        "conv_rnn_b": nrm((DEPTH, D_RNN), 0.02),
        "w_a": nrm((DEPTH, 2, RNN_HEADS, RNN_HEAD_DIM, RNN_HEAD_DIM), RNN_HEAD_DIM ** -0.5),
        "b_a": nrm((DEPTH, 2, D_RNN), 0.02),
        "w_x": nrm((DEPTH, 2, RNN_HEADS, RNN_HEAD_DIM, RNN_HEAD_DIM), RNN_HEAD_DIM ** -0.5),
        "b_x": nrm((DEPTH, 2, D_RNN), 0.02),
        "lam": jnp.log(a0) - jnp.log1p(-a0),
        "w_rnn_proj": nrm((DEPTH, D_RNN, D_MODEL), D_RNN ** -0.5),
        "conv_dw_w": nrm((DEPTH, CONV_WIDTH, D_CONV), CONV_WIDTH ** -0.5),
        "conv_dw_b": nrm((DEPTH, D_CONV), 0.02),
        "ln_g": gain((DEPTH, D_CONV)),
        "ln_b": nrm((DEPTH, D_CONV), 0.02),
        "w_conv_proj": nrm((DEPTH, D_CONV, D_MODEL), D_CONV ** -0.5),
        "b_conv_proj": nrm((DEPTH, D_MODEL), 0.02),
        "w_out": nrm((DEPTH, D_MODEL, D_MODEL), D_MODEL ** -0.5),
        "norm_ffn_g": gain((DEPTH, D_MODEL)),
        "w_router": nrm((DEPTH, D_MODEL, N_EXPERTS), D_MODEL ** -0.5),
        "b_router": nrm((DEPTH, N_EXPERTS), 0.01),
        "w_gu": nrm((DEPTH, N_EXPERTS, D_MODEL, 2 * D_EXPERT), D_MODEL ** -0.5),
        "b_gu": nrm((DEPTH, N_EXPERTS, 2 * D_EXPERT), 0.02),
        "w_down": nrm((DEPTH, N_EXPERTS, D_EXPERT, D_MODEL), D_EXPERT ** -0.5),
        "b_down": nrm((DEPTH, N_EXPERTS, D_MODEL), 0.02),
        "norm_ple_g": gain((DEPTH, D_MODEL)),
        "w_ple_gate": nrm((DEPTH, D_MODEL, D_MODEL), D_MODEL ** -0.5),
        "w_ple_proj": nrm((DEPTH, D_PLE, D_MODEL), D_PLE ** -0.5),
        "ple_post_g": gain((DEPTH, D_MODEL)),
        "final_g": gain((D_MODEL,)),
    }


def reference(x_prompt, x_sample, p_prompt, p_sample, norm_mix_g, w_in, b_in, conv_rnn_w, conv_rnn_b,
              w_a, b_a, w_x, b_x, lam, w_rnn_proj, conv_dw_w, conv_dw_b, ln_g, ln_b, w_conv_proj,
              b_conv_proj, w_out, norm_ffn_g, w_router, b_router, w_gu, b_gu, w_down, b_down,
              norm_ple_g, w_ple_gate, w_ple_proj, ple_post_g, final_g):
    weights = (norm_mix_g, w_in, b_in, conv_rnn_w, conv_rnn_b, w_a, b_a, w_x, b_x, lam, w_rnn_proj,
               conv_dw_w, conv_dw_b, ln_g, ln_b, w_conv_proj, b_conv_proj, w_out,
               norm_ffn_g, w_router, b_router, w_gu, b_gu, w_down, b_down,
               norm_ple_g, w_ple_gate, w_ple_proj, ple_post_g, final_g)
    y_prompt = trunk(x_prompt, p_prompt, weights)
    y_sample = trunk(x_sample, p_sample, weights)
    return (y_prompt, y_sample)
```

```python
import functools

import jax
import jax.numpy as jnp
from jax import lax
from jax.experimental import pallas as pl
from jax.experimental.pallas import tpu as pltpu

D_MODEL = 1024
D_IN = 6 * D_MODEL
RNN_HEADS = 16
RNN_HEAD_DIM = D_MODEL // RNN_HEADS
RNN_CONV = 4
RG_C = 8.0
CONV_WIDTH = 31
N_EXPERTS = 32
TOP_K = 4
D_EXPERT = D_MODEL
SWIGLU_LIMIT = 7.0
SWIGLU_ALPHA = 1.702
D_PLE = 256
EPS = 1e-6

LANES = 128
SUBLANES = 8
NEG_BIG = -1e30

F32 = jnp.float32
BF16 = jnp.bfloat16

VMEM_LIMIT = 56 * 1024 * 1024


def _cparams(sem):
    return pltpu.CompilerParams(dimension_semantics=sem, vmem_limit_bytes=VMEM_LIMIT)


def _rms(x, g):
    return x * lax.rsqrt(jnp.mean(x * x, axis=-1, keepdims=True) + EPS) * g


def _inproj_kernel(x_ref, g_ref, w_ref, b_ref, z_ref, h_scr):
    @pl.when(pl.program_id(1) == 0)
    def _():
        h_scr[...] = _rms(x_ref[...], g_ref[...]).astype(BF16)

    z_ref[...] = jnp.dot(h_scr[...], w_ref[...], preferred_element_type=F32) + b_ref[...]


def _inproj(x, g, w, b, tm=1024, tn=2048):
    t = x.shape[0]
    return pl.pallas_call(
        _inproj_kernel,
        grid=(t // tm, D_IN // tn),
        in_specs=[
            pl.BlockSpec((tm, D_MODEL), lambda i, j: (i, 0)),
            pl.BlockSpec((1, D_MODEL), lambda i, j: (0, 0)),
            pl.BlockSpec((D_MODEL, tn), lambda i, j: (0, j)),
            pl.BlockSpec((1, tn), lambda i, j: (0, j)),
        ],
        out_specs=pl.BlockSpec((tm, tn), lambda i, j: (i, j)),
        out_shape=jax.ShapeDtypeStruct((t, D_IN), F32),
        scratch_shapes=[pltpu.VMEM((tm, D_MODEL), BF16)],
        compiler_params=_cparams(("arbitrary", "arbitrary")),
        name="inproj",
    )(x, g, w, b)


RNN_CB = 256
RNN_CHUNK = 128


def _rnn_kernel(xr_ref, yr_ref, cw_ref, cb_ref, wa_ref, ba_ref, wx_ref, bx_ref, c_ref, o_ref, xpad, hf):
    s, cb = xr_ref.shape
    chunk = RNN_CHUNK
    nchunks = s // chunk
    zeros8 = jnp.zeros((SUBLANES, cb), F32)
    xpad[0:SUBLANES, :] = zeros8
    xpad[s + SUBLANES:s + 2 * SUBLANES, :] = zeros8
    xpad[SUBLANES:s + SUBLANES, :] = xr_ref[...]
    row8 = lax.broadcasted_iota(jnp.int32, (SUBLANES, cb), 0)

    def conv(t0):
        xe = xpad[pl.ds(t0, chunk + 2 * SUBLANES), :]
        acc = cb_ref[...] + cw_ref[0:1, :] * xe[6:6 + chunk]
        for k in range(1, RNN_CONV):
            acc = acc + cw_ref[k:k + 1, :] * xe[6 + k:6 + k + chunk]
        return acc

    def gates(xc, d):
        xb = xc.astype(BF16)
        r = jax.nn.sigmoid(jnp.dot(xb, wa_ref[d], preferred_element_type=F32) + ba_ref[d:d + 1, :])
        ig = jax.nn.sigmoid(jnp.dot(xb, wx_ref[d], preferred_element_type=F32) + bx_ref[d:d + 1, :])
        a = jnp.exp(-c_ref[d:d + 1, :] * r)
        u = jnp.sqrt(1.0 - a * a) * (ig * xc)
        return a, u

    def tile_scan(a, u, reverse):
        for dist in (1, 2, 4):
            if reverse:
                keep = row8 < SUBLANES - dist
                shift = SUBLANES - dist
            else:
                keep = row8 >= dist
                shift = dist
            us = jnp.where(keep, pltpu.roll(u, shift, 0), 0.0)
            a_s = jnp.where(keep, pltpu.roll(a, shift, 0), 1.0)
            u = u + a * us
            a = a * a_s
        return a, u

    def fwd_body(i, h):
        t0 = pl.multiple_of(i * chunk, chunk)
        a, u = gates(conv(t0), 0)
        tiles = []
        for j in range(chunk // SUBLANES):
            at, ut = tile_scan(a[j * 8:(j + 1) * 8], u[j * 8:(j + 1) * 8], False)
            ht = ut + at * h
            h = ht[7:8, :]
            tiles.append(ht)
        hf[pl.ds(t0, chunk), :] = jnp.concatenate(tiles, axis=0)
        return h

    lax.fori_loop(0, nchunks, fwd_body, jnp.zeros((1, cb), F32))

    def bwd_body(i, h):
        t0 = pl.multiple_of((nchunks - 1 - i) * chunk, chunk)
        a, u = gates(conv(t0), 1)
        tiles = [None] * (chunk // SUBLANES)
        for j in reversed(range(chunk // SUBLANES)):
            at, ut = tile_scan(a[j * 8:(j + 1) * 8], u[j * 8:(j + 1) * 8], True)
            ht = ut + at * h
            h = ht[0:1, :]
            tiles[j] = ht
        hb = jnp.concatenate(tiles, axis=0)
        y = yr_ref[pl.ds(t0, chunk), :]
        o_ref[pl.ds(t0, chunk), :] = ((hf[pl.ds(t0, chunk), :] + hb) * jax.nn.gelu(y)).astype(o_ref.dtype)
        return h

    lax.fori_loop(0, nchunks, bwd_body, jnp.zeros((1, cb), F32))


def _rnn(z3, cw, cb_, wa_bd, ba, wx_bd, bx, c):
    b, s, _ = z3.shape
    nc = D_MODEL // RNN_CB
    return pl.pallas_call(
        _rnn_kernel,
        grid=(b, nc),
        in_specs=[
            pl.BlockSpec((None, s, RNN_CB), lambda i, j: (i, 0, j)),
            pl.BlockSpec((None, s, RNN_CB), lambda i, j: (i, 0, nc + j)),
            pl.BlockSpec((RNN_CONV, RNN_CB), lambda i, j: (0, j)),
            pl.BlockSpec((1, RNN_CB), lambda i, j: (0, j)),
            pl.BlockSpec((2, None, RNN_CB, RNN_CB), lambda i, j: (0, j, 0, 0)),
            pl.BlockSpec((2, RNN_CB), lambda i, j: (0, j)),
            pl.BlockSpec((2, None, RNN_CB, RNN_CB), lambda i, j: (0, j, 0, 0)),
            pl.BlockSpec((2, RNN_CB), lambda i, j: (0, j)),
            pl.BlockSpec((2, RNN_CB), lambda i, j: (0, j)),
        ],
        out_specs=pl.BlockSpec((None, s, RNN_CB), lambda i, j: (i, 0, j)),
        out_shape=jax.ShapeDtypeStruct((b, s, D_MODEL), BF16),
        scratch_shapes=[pltpu.VMEM((s + 2 * SUBLANES, RNN_CB), F32), pltpu.VMEM((s, RNN_CB), F32)],
        compiler_params=_cparams(("arbitrary", "arbitrary")),
        name="rnn",
    )(z3, z3, cw, cb_, wa_bd, ba, wx_bd, bx, c)


MIX_TS = 256
MIX_HALO = 16
MIX_RC = 32
MIX_LC = 256


def _mix_kernel(cv_ref, cg_ref, cvp_ref, cgp_ref, cvn_ref, cgn_ref, g_ref, gr_ref, gc_ref, x_ref,
                dw_ref, db_ref, lng_ref, lnb_ref, wcp_ref, bcp_ref, wrp_ref, wo_ref, nfg_ref, wr_ref, br_ref,
                x1_ref, h2_ref, ridx_ref, rgate_ref, uext, cscr):
    ts = cv_ref.shape[0]
    i = pl.program_id(1)
    last = pl.num_programs(1) - 1
    up = cvp_ref[...] * jax.nn.sigmoid(cgp_ref[...])
    un = cvn_ref[...] * jax.nn.sigmoid(cgn_ref[...])
    uext[0:MIX_HALO, :] = jnp.where(i > 0, up, 0.0)
    uext[MIX_HALO:MIX_HALO + ts, :] = cv_ref[...] * jax.nn.sigmoid(cg_ref[...])
    uext[MIX_HALO + ts:2 * MIX_HALO + ts, :] = jnp.where(i < last, un, 0.0)

    off = MIX_HALO - CONV_WIDTH // 2

    def conv_chunk(rc, carry):
        r0 = pl.multiple_of(rc * MIX_RC, MIX_RC)
        for lc in range(D_MODEL // MIX_LC):
            lanes = slice(lc * MIX_LC, (lc + 1) * MIX_LC)
            win = uext[pl.ds(r0, MIX_RC + 2 * MIX_HALO), lanes]
            acc = db_ref[:, lanes] + dw_ref[0:1, lanes] * win[off:off + MIX_RC]
            for k in range(1, CONV_WIDTH):
                acc = acc + dw_ref[k:k + 1, lanes] * win[off + k:off + k + MIX_RC]
            cscr[pl.ds(r0, MIX_RC), lanes] = acc
        return carry

    lax.fori_loop(0, ts // MIX_RC, conv_chunk, 0)

    c = cscr[...]
    mu = jnp.mean(c, axis=-1, keepdims=True)
    cc = c - mu
    var = jnp.mean(cc * cc, axis=-1, keepdims=True)
    y = cc * lax.rsqrt(var + EPS) * lng_ref[...] + lnb_ref[...]
    sw = y * jax.nn.sigmoid(y)
    conv = jnp.dot(sw.astype(BF16), wcp_ref[...], preferred_element_type=F32) + bcp_ref[...]
    rnn = jnp.dot(g_ref[...], wrp_ref[...], preferred_element_type=F32)
    merged = jax.nn.sigmoid(gr_ref[...]) * rnn + jax.nn.sigmoid(gc_ref[...]) * conv
    x1 = x_ref[...] + jnp.dot(merged.astype(BF16), wo_ref[...], preferred_element_type=F32)
    x1_ref[...] = x1
    h2 = _rms(x1, nfg_ref[...])
    h2_ref[...] = h2

    logits = jnp.dot(h2.astype(BF16), wr_ref[...], preferred_element_type=F32) + br_ref[...]
    lane = lax.broadcasted_iota(jnp.int32, logits.shape, 1)
    vals, idxs = [], []
    for _ in range(TOP_K):
        m = jnp.max(logits, axis=-1, keepdims=True)
        idx = jnp.min(jnp.where(logits == m, lane, LANES), axis=-1, keepdims=True)
        vals.append(m)
        idxs.append(idx)
        logits = jnp.where(lane == idx, 2.0 * NEG_BIG, logits)
    es = [jnp.exp(v - vals[0]) for v in vals]
    inv = 1.0 / (es[0] + es[1] + es[2] + es[3])
    ridx = jnp.zeros(logits.shape, jnp.int32)
    rgate = jnp.zeros(logits.shape, F32)
    for k in range(TOP_K):
        ridx = jnp.where(lane == k, idxs[k], ridx)
        rgate = jnp.where(lane == k, es[k] * inv, rgate)
    ridx_ref[...] = ridx
    rgate_ref[...] = rgate


def _mix(z3, g3, x3, dw, db, lng, lnb, wcp, bcp, wrp, wo, nfg, wr, br):
    b, s, _ = z3.shape
    ts = MIX_TS
    nt = s // ts
    hb = ts // MIX_HALO
    nhb = s // MIX_HALO
    tile = lambda col: pl.BlockSpec((None, ts, D_MODEL), lambda i, j, col=col: (i, j, col))
    prev = lambda col: pl.BlockSpec((None, MIX_HALO, D_MODEL),
                                    lambda i, j, col=col: (i, jnp.maximum(j * hb - 1, 0), col))
    nxt = lambda col: pl.BlockSpec((None, MIX_HALO, D_MODEL),
                                   lambda i, j, col=col: (i, jnp.minimum((j + 1) * hb, nhb - 1), col))
    full = lambda shape: pl.BlockSpec(shape, lambda i, j: (0,) * len(shape))
    out_tile = lambda w: pl.BlockSpec((None, ts, w), lambda i, j: (i, j, 0))
    return pl.pallas_call(
        _mix_kernel,
        grid=(b, nt),
        in_specs=[
            tile(2), tile(3), prev(2), prev(3), nxt(2), nxt(3),
            tile(0),
            tile(4), tile(5),
            tile(0),
            full((CONV_WIDTH, D_MODEL)), full((1, D_MODEL)), full((1, D_MODEL)), full((1, D_MODEL)),
            full((D_MODEL, D_MODEL)), full((1, D_MODEL)), full((D_MODEL, D_MODEL)), full((D_MODEL, D_MODEL)),
            full((1, D_MODEL)), full((D_MODEL, LANES)), full((1, LANES)),
        ],
        out_specs=[out_tile(D_MODEL), out_tile(D_MODEL), out_tile(LANES), out_tile(LANES)],
        out_shape=[
            jax.ShapeDtypeStruct((b, s, D_MODEL), F32),
            jax.ShapeDtypeStruct((b, s, D_MODEL), F32),
            jax.ShapeDtypeStruct((b, s, LANES), jnp.int32),
            jax.ShapeDtypeStruct((b, s, LANES), F32),
        ],
        scratch_shapes=[pltpu.VMEM((ts + 2 * MIX_HALO, D_MODEL), F32), pltpu.VMEM((ts, D_MODEL), F32)],
        compiler_params=_cparams(("arbitrary", "arbitrary")),
        name="mix",
    )(z3, z3, z3, z3, z3, z3, g3, z3, z3, x3, dw, db, lng, lnb, wcp, bcp, wrp, wo, nfg, wr, br)


MOE_BM = 256


def _moe_kernel(be_ref, bn_ref, src_ref, srcn_ref, h_hbm, wgu_ref, bgu_ref, wd_ref, bd_ref, out_hbm,
                xbuf, ybuf, gsem, ssem, *, n_tok):
    del be_ref
    bm = xbuf.shape[1]
    i = pl.program_id(0)
    nb = pl.num_programs(0)
    slot = i % 2

    def round8(n):
        return pl.multiple_of((n + (SUBLANES - 1)) & (-SUBLANES), SUBLANES)

    def gather_row(src, r, s):
        tok = lax.shift_right_logical(jnp.maximum(src[0, 0, r], 0), 2)
        return pltpu.make_async_copy(h_hbm.at[pl.ds(tok, 1)], xbuf.at[s, pl.ds(r, 1)], gsem.at[s])

    def scatter_row(r, s):
        f = src_ref[0, 0, r]
        real = (f & (TOP_K - 1)) * n_tok + lax.shift_right_logical(f, 2)
        dump = TOP_K * n_tok + s * SUBLANES + (r & (SUBLANES - 1))
        dst = jnp.where(f >= 0, real, dump)
        return pltpu.make_async_copy(ybuf.at[s, pl.ds(r, 1)], out_hbm.at[pl.ds(dst, 1)], ssem.at[s])

    def issue_gather(src, n, s):
        def body(r, c):
            gather_row(src, r, s).start()
            return c
        lax.fori_loop(0, round8(n), body, 0)

    def wait_gather(n, s):
        n8 = round8(n)
        pltpu.make_async_copy(h_hbm.at[pl.ds(0, n8)], xbuf.at[s, pl.ds(0, n8)], gsem.at[s]).wait()

    def wait_scatter(n, s):
        n8 = round8(n)
        pltpu.make_async_copy(ybuf.at[s, pl.ds(0, n8)], out_hbm.at[pl.ds(0, n8)], ssem.at[s]).wait()

    @pl.when(i == 0)
    def _():
        xbuf[...] = jnp.zeros(xbuf.shape, xbuf.dtype)
        init_dump = pltpu.make_async_copy(xbuf.at[0, pl.ds(0, 2 * SUBLANES)],
                                          out_hbm.at[pl.ds(TOP_K * n_tok, 2 * SUBLANES)], ssem.at[0])
        init_dump.start()
        init_dump.wait()
        issue_gather(src_ref, bn_ref[0], 0)

    @pl.when(i + 1 < nb)
    def _():
        issue_gather(srcn_ref, bn_ref[i + 1], 1 - slot)

    n_old = bn_ref[jnp.maximum(i - 2, 0)]

    @pl.when(jnp.logical_and(i >= 2, n_old > 0))
    def _():
        wait_scatter(n_old, slot)

    n = bn_ref[i]

    @pl.when(n > 0)
    def _():
        wait_gather(n, slot)
        xb = xbuf[slot].astype(BF16)
        gu = jnp.dot(xb, wgu_ref[...], preferred_element_type=F32) + bgu_ref[...]
        gt = jnp.minimum(gu[:, :D_EXPERT], SWIGLU_LIMIT)
        upv = jnp.clip(gu[:, D_EXPERT:], -SWIGLU_LIMIT, SWIGLU_LIMIT)
        act = (upv + 1.0) * (gt * jax.nn.sigmoid(gt * SWIGLU_ALPHA))
        ybuf[slot] = jnp.dot(act.astype(BF16), wd_ref[...], preferred_element_type=F32) + bd_ref[...]

        def body(r, c):
            scatter_row(r, slot).start()
            return c
        lax.fori_loop(0, round8(n), body, 0)

    @pl.when(i == nb - 1)
    def _():
        n_prev = bn_ref[jnp.maximum(i - 1, 0)]

        @pl.when(jnp.logical_and(i >= 1, n_prev > 0))
        def _():
            wait_scatter(n_prev, 1 - slot)

        @pl.when(n > 0)
        def _():
            wait_scatter(n, slot)


def _moe(h2, block_e, block_n, row_src, wgu, bgu, wd, bd):
    t = h2.shape[0]
    nb = block_e.shape[0]
    bm = MOE_BM
    grid_spec = pltpu.PrefetchScalarGridSpec(
        num_scalar_prefetch=2,
        grid=(nb,),
        in_specs=[
            pl.BlockSpec((1, 1, bm), lambda i, be, bn: (i, 0, 0), memory_space=pltpu.SMEM),
            pl.BlockSpec((1, 1, bm), lambda i, be, bn: (jnp.minimum(i + 1, nb - 1), 0, 0),
                         memory_space=pltpu.SMEM),
            pl.BlockSpec(memory_space=pl.ANY),
            pl.BlockSpec((None, D_MODEL, 2 * D_EXPERT), lambda i, be, bn: (be[i], 0, 0)),
            pl.BlockSpec((None, 1, 2 * D_EXPERT), lambda i, be, bn: (be[i], 0, 0)),
            pl.BlockSpec((None, D_EXPERT, D_MODEL), lambda i, be, bn: (be[i], 0, 0)),
            pl.BlockSpec((None, 1, D_MODEL), lambda i, be, bn: (be[i], 0, 0)),
        ],
        out_specs=pl.BlockSpec(memory_space=pl.ANY),
        scratch_shapes=[
            pltpu.VMEM((2, bm, D_MODEL), F32),
            pltpu.VMEM((2, bm, D_MODEL), F32),
            pltpu.SemaphoreType.DMA((2,)),
            pltpu.SemaphoreType.DMA((2,)),
        ],
    )
    return pl.pallas_call(
        functools.partial(_moe_kernel, n_tok=t),
        grid_spec=grid_spec,
        out_shape=jax.ShapeDtypeStruct((TOP_K * t + 2 * SUBLANES, D_MODEL), F32),
        compiler_params=_cparams(("arbitrary",)),
        name="moe",
    )(block_e, block_n, row_src, row_src, h2, wgu, bgu, wd, bd)


def _route(ridx, bm):
    t = ridx.shape[0]
    tk = t * TOP_K
    flat_e = ridx[:, :TOP_K].reshape(-1)
    order = jnp.argsort(flat_e, stable=True).astype(jnp.int32)
    counts = jnp.bincount(flat_e, length=N_EXPERTS).astype(jnp.int32)
    padded = (counts + bm - 1) // bm * bm
    pad_end = jnp.cumsum(padded)
    pad_start = pad_end - padded
    start = jnp.cumsum(counts) - counts
    n_blocks = tk // bm + N_EXPERTS
    blk = jnp.arange(n_blocks, dtype=jnp.int32) * bm
    block_e = jnp.minimum(jnp.searchsorted(pad_end, blk, side="right"), N_EXPERTS - 1).astype(jnp.int32)
    block_n = jnp.clip(counts[block_e] - (blk - pad_start[block_e]), 0, bm).astype(jnp.int32)
    rows = jnp.arange(n_blocks * bm, dtype=jnp.int32)
    e_r = jnp.repeat(block_e, bm)
    within = rows - pad_start[e_r]
    valid = within < counts[e_r]
    j = jnp.clip(start[e_r] + within, 0, tk - 1)
    row_src = jnp.where(valid, order[j], -1).astype(jnp.int32)
    return block_e, block_n, row_src.reshape(n_blocks, 1, bm)


FIN_TM = 256


def _final_kernel(x1_ref, y0_ref, y1_ref, y2_ref, y3_ref, rg_ref, p_ref, npg_ref, wpg_ref, wpp_ref, ppg_ref,
                  fg_ref, o_ref):
    rg = rg_ref[...]
    x2 = x1_ref[...]
    for k, y_ref in enumerate((y0_ref, y1_ref, y2_ref, y3_ref)):
        x2 = x2 + rg[:, k:k + 1] * y_ref[...]
    h = _rms(x2, npg_ref[...])
    gate = jax.nn.sigmoid(jnp.dot(h.astype(BF16), wpg_ref[...], preferred_element_type=F32))
    e = _rms(jnp.dot(p_ref[...].astype(BF16), wpp_ref[...], preferred_element_type=F32), ppg_ref[...])
    x3 = x2 + gate * e
    o_ref[...] = _rms(x3, fg_ref[...])


def _final(x1, ys, rgate, p, npg, wpg, wpp, ppg, fg):
    t = x1.shape[0]
    tm = FIN_TM
    nt = t // tm
    full = lambda shape: pl.BlockSpec(shape, lambda i: (0,) * len(shape))
    slab = lambda k: pl.BlockSpec((tm, D_MODEL), lambda i, k=k: (k * nt + i, 0))
    return pl.pallas_call(
        _final_kernel,
        grid=(nt,),
        in_specs=[
            pl.BlockSpec((tm, D_MODEL), lambda i: (i, 0)),
            slab(0), slab(1), slab(2), slab(3),
            pl.BlockSpec((tm, LANES), lambda i: (i, 0)),
            pl.BlockSpec((tm, D_PLE), lambda i: (i, 0)),
            full((1, D_MODEL)), full((D_MODEL, D_MODEL)), full((D_PLE, D_MODEL)), full((1, D_MODEL)),
            full((1, D_MODEL)),
        ],
        out_specs=pl.BlockSpec((tm, D_MODEL), lambda i: (i, 0)),
        out_shape=jax.ShapeDtypeStruct((t, D_MODEL), F32),
        compiler_params=_cparams(("arbitrary",)),
        name="final",
    )(x1, ys, ys, ys, ys, rgate, p, npg, wpg, wpp, ppg, fg)


def _block_diag(w):
    per = RNN_CB // RNN_HEAD_DIM
    groups = D_MODEL // RNN_CB
    w = w.reshape(2, groups, per, RNN_HEAD_DIM, RNN_HEAD_DIM)
    eye = jnp.eye(per, dtype=w.dtype)
    bd = jnp.einsum("dgpij,pq->dgpiqj", w, eye)
    return bd.reshape(2, groups, RNN_CB, RNN_CB)


def _trunk(x, p, wts):
    b, s, d = x.shape
    t = b * s
    row = lambda v: v.reshape(1, -1)
    z = _inproj(x.reshape(t, d), row(wts["norm_mix_g"]), wts["w_in"], row(wts["b_in"]))
    z3 = z.reshape(b, s, D_IN)
    g3 = _rnn(z3, wts["conv_rnn_w"], row(wts["conv_rnn_b"]), wts["w_a_bd"], wts["b_a"], wts["w_x_bd"], wts["b_x"],
              wts["rg_c"])
    x1, h2, ridx, rgate = _mix(z3, g3, x, wts["conv_dw_w"], row(wts["conv_dw_b"]), row(wts["ln_g"]),
                               row(wts["ln_b"]), wts["w_conv_proj"], row(wts["b_conv_proj"]), wts["w_rnn_proj"],
                               wts["w_out"], row(wts["norm_ffn_g"]), wts["w_router"], wts["b_router"])
    ridx = ridx.reshape(t, LANES)
    block_e, block_n, row_src = _route(ridx, MOE_BM)
    ys = _moe(h2.reshape(t, d), block_e, block_n, row_src, wts["w_gu"], wts["b_gu"], wts["w_down"], wts["b_down"])
    y = _final(x1.reshape(t, d), ys, rgate.reshape(t, LANES), p.reshape(t, D_PLE), row(wts["norm_ple_g"]),
               wts["w_ple_gate"], wts["w_ple_proj"], row(wts["ple_post_g"]), row(wts["final_g"]))
    return y.reshape(b, s, d)


def kernel(x_prompt, x_sample, p_prompt, p_sample, norm_mix_g, w_in, b_in, conv_rnn_w, conv_rnn_b, w_a, b_a, w_x,
           b_x, lam, w_rnn_proj, conv_dw_w, conv_dw_b, ln_g, ln_b, w_conv_proj, b_conv_proj, w_out, norm_ffn_g,
           w_router, b_router, w_gu, b_gu, w_down, b_down, norm_ple_g, w_ple_gate, w_ple_proj, ple_post_g, final_g):
    l = 0
    wts = {
        "norm_mix_g": norm_mix_g[l], "w_in": w_in[l].astype(BF16), "b_in": b_in[l],
        "conv_rnn_w": conv_rnn_w[l], "conv_rnn_b": conv_rnn_b[l],
        "w_a_bd": _block_diag(w_a[l]).astype(BF16), "b_a": b_a[l],
        "w_x_bd": _block_diag(w_x[l]).astype(BF16), "b_x": b_x[l],
        "rg_c": RG_C * jax.nn.softplus(-lam[l]),
        "w_rnn_proj": w_rnn_proj[l].astype(BF16),
        "conv_dw_w": conv_dw_w[l], "conv_dw_b": conv_dw_b[l], "ln_g": ln_g[l], "ln_b": ln_b[l],
        "w_conv_proj": w_conv_proj[l].astype(BF16), "b_conv_proj": b_conv_proj[l],
        "w_out": w_out[l].astype(BF16), "norm_ffn_g": norm_ffn_g[l],
        "w_router": jnp.pad(w_router[l], ((0, 0), (0, LANES - N_EXPERTS))).astype(BF16),
        "b_router": jnp.pad(b_router[l], (0, LANES - N_EXPERTS), constant_values=NEG_BIG).reshape(1, LANES),
        "w_gu": w_gu[l].astype(BF16), "b_gu": b_gu[l].reshape(N_EXPERTS, 1, 2 * D_EXPERT),
        "w_down": w_down[l].astype(BF16), "b_down": b_down[l].reshape(N_EXPERTS, 1, D_MODEL),
        "norm_ple_g": norm_ple_g[l], "w_ple_gate": w_ple_gate[l].astype(BF16),
        "w_ple_proj": w_ple_proj[l].astype(BF16), "ple_post_g": ple_post_g[l], "final_g": final_g,
    }
    y_prompt = _trunk(x_prompt, p_prompt[l], wts)
    y_sample = _trunk(x_sample, p_sample[l], wts)
    return (y_prompt, y_sample)
```

```python
import functools

import jax
import jax.numpy as jnp
from jax import lax
from jax.experimental import pallas as pl
from jax.experimental.pallas import tpu as pltpu

D_MODEL = 1024
D_IN = 6 * D_MODEL
RNN_HEADS = 16
RNN_HEAD_DIM = D_MODEL // RNN_HEADS
RNN_CONV = 4
RG_C = 8.0
CONV_WIDTH = 31
N_EXPERTS = 32
TOP_K = 4
D_EXPERT = D_MODEL
SWIGLU_LIMIT = 7.0
SWIGLU_ALPHA = 1.702
D_PLE = 256
EPS = 1e-6

LANES = 128
SUBLANES = 8
NEG_BIG = -1e30

F32 = jnp.float32
BF16 = jnp.bfloat16

VMEM_LIMIT = 56 * 1024 * 1024


def _cparams(sem):
    return pltpu.CompilerParams(dimension_semantics=sem, vmem_limit_bytes=VMEM_LIMIT)


def _rms(x, g):
    return x * lax.rsqrt(jnp.mean(x * x, axis=-1, keepdims=True) + EPS) * g


def _sigmoid(x):
    return 0.5 * jnp.tanh(0.5 * x) + 0.5


def _inproj_kernel(x_ref, g_ref, w_ref, b_ref, z_ref, h_scr):
    @pl.when(pl.program_id(1) == 0)
    def _():
        h_scr[...] = _rms(x_ref[...], g_ref[...]).astype(BF16)

    z_ref[...] = jnp.dot(h_scr[...], w_ref[...], preferred_element_type=F32) + b_ref[...]


def _inproj(x, g, w, b, tm=1024, tn=2048):
    t = x.shape[0]
    return pl.pallas_call(
        _inproj_kernel,
        grid=(t // tm, D_IN // tn),
        in_specs=[
            pl.BlockSpec((tm, D_MODEL), lambda i, j: (i, 0)),
            pl.BlockSpec((1, D_MODEL), lambda i, j: (0, 0)),
            pl.BlockSpec((D_MODEL, tn), lambda i, j: (0, j)),
            pl.BlockSpec((1, tn), lambda i, j: (0, j)),
        ],
        out_specs=pl.BlockSpec((tm, tn), lambda i, j: (i, j)),
        out_shape=jax.ShapeDtypeStruct((t, D_IN), F32),
        scratch_shapes=[pltpu.VMEM((tm, D_MODEL), BF16)],
        compiler_params=_cparams(("arbitrary", "arbitrary")),
        name="inproj",
    )(x, g, w, b)


RNN_CB = 256
RNN_CHUNK = 128


def _rnn_kernel(xr_ref, yr_ref, cw_ref, cb_ref, wa_ref, ba_ref, wx_ref, bx_ref, c_ref, o_ref, xpad, hf):
    s, cb = xr_ref.shape
    chunk = RNN_CHUNK
    nchunks = s // chunk
    zeros8 = jnp.zeros((SUBLANES, LANES), F32)
    for c in range(cb // LANES):
        xpad[c, 0:SUBLANES, :] = zeros8
        xpad[c, s + SUBLANES:s + 2 * SUBLANES, :] = zeros8
        xpad[c, SUBLANES:s + SUBLANES, :] = xr_ref[:, c * LANES:(c + 1) * LANES]
    row8 = lax.broadcasted_iota(jnp.int32, (SUBLANES, cb), 0)

    def conv(t0):
        parts = []
        for c in range(cb // LANES):
            lanes = slice(c * LANES, (c + 1) * LANES)
            acc = cb_ref[:, lanes] + cw_ref[0:1, lanes] * xpad[c, pl.ds(t0 + 6, chunk), :]
            for k in range(1, RNN_CONV):
                acc = acc + cw_ref[k:k + 1, lanes] * xpad[c, pl.ds(t0 + (6 + k), chunk), :]
            parts.append(acc)
        return jnp.concatenate(parts, axis=1)

    def gates(xc, d):
        xb = xc.astype(BF16)
        r = _sigmoid(jnp.dot(xb, wa_ref[d], preferred_element_type=F32) + ba_ref[d:d + 1, :])
        ig = _sigmoid(jnp.dot(xb, wx_ref[d], preferred_element_type=F32) + bx_ref[d:d + 1, :])
        a = jnp.exp(-c_ref[d:d + 1, :] * r)
        u = jnp.sqrt(1.0 - a * a) * (ig * xc)
        return a, u

    def tile_scan(a, u, reverse):
        for dist in (1, 2, 4):
            if reverse:
                keep = row8 < SUBLANES - dist
                shift = SUBLANES - dist
            else:
                keep = row8 >= dist
                shift = dist
            us = jnp.where(keep, pltpu.roll(u, shift, 0), 0.0)
            a_s = jnp.where(keep, pltpu.roll(a, shift, 0), 1.0)
            u = u + a * us
            a = a * a_s
        return a, u

    def fwd_body(i, h):
        t0 = pl.multiple_of(i * chunk, chunk)
        a, u = gates(conv(t0), 0)
        tiles = []
        for j in range(chunk // SUBLANES):
            at, ut = tile_scan(a[j * 8:(j + 1) * 8], u[j * 8:(j + 1) * 8], False)
            ht = ut + at * h
            h = ht[7:8, :]
            tiles.append(ht)
        hf[pl.ds(t0, chunk), :] = jnp.concatenate(tiles, axis=0)
        return h

    lax.fori_loop(0, nchunks, fwd_body, jnp.zeros((1, cb), F32))

    def bwd_body(i, h):
        t0 = pl.multiple_of((nchunks - 1 - i) * chunk, chunk)
        a, u = gates(conv(t0), 1)
        tiles = [None] * (chunk // SUBLANES)
        for j in reversed(range(chunk // SUBLANES)):
            at, ut = tile_scan(a[j * 8:(j + 1) * 8], u[j * 8:(j + 1) * 8], True)
            ht = ut + at * h
            h = ht[0:1, :]
            tiles[j] = ht
        hb = jnp.concatenate(tiles, axis=0)
        y = yr_ref[pl.ds(t0, chunk), :]
        o_ref[pl.ds(t0, chunk), :] = ((hf[pl.ds(t0, chunk), :] + hb) * jax.nn.gelu(y)).astype(o_ref.dtype)
        return h

    lax.fori_loop(0, nchunks, bwd_body, jnp.zeros((1, cb), F32))


def _rnn(z3, cw, cb_, wa_bd, ba, wx_bd, bx, c):
    b, s, _ = z3.shape
    nc = D_MODEL // RNN_CB
    return pl.pallas_call(
        _rnn_kernel,
        grid=(b, nc),
        in_specs=[
            pl.BlockSpec((None, s, RNN_CB), lambda i, j: (i, 0, j)),
            pl.BlockSpec((None, s, RNN_CB), lambda i, j: (i, 0, nc + j)),
            pl.BlockSpec((RNN_CONV, RNN_CB), lambda i, j: (0, j)),
            pl.BlockSpec((1, RNN_CB), lambda i, j: (0, j)),
            pl.BlockSpec((2, None, RNN_CB, RNN_CB), lambda i, j: (0, j, 0, 0)),
            pl.BlockSpec((2, RNN_CB), lambda i, j: (0, j)),
            pl.BlockSpec((2, None, RNN_CB, RNN_CB), lambda i, j: (0, j, 0, 0)),
            pl.BlockSpec((2, RNN_CB), lambda i, j: (0, j)),
            pl.BlockSpec((2, RNN_CB), lambda i, j: (0, j)),
        ],
        out_specs=pl.BlockSpec((None, s, RNN_CB), lambda i, j: (i, 0, j)),
        out_shape=jax.ShapeDtypeStruct((b, s, D_MODEL), BF16),
        scratch_shapes=[pltpu.VMEM((RNN_CB // LANES, s + 2 * SUBLANES, LANES), F32),
                        pltpu.VMEM((s, RNN_CB), F32)],
        compiler_params=_cparams(("arbitrary", "arbitrary")),
        name="rnn",
    )(z3, z3, cw, cb_, wa_bd, ba, wx_bd, bx, c)


MIX_TS = 256
MIX_HALO = 16
MIX_RC = 64


def _mix_kernel(cv_ref, cg_ref, cvp_ref, cgp_ref, cvn_ref, cgn_ref, g_ref, gr_ref, gc_ref, x_ref,
                dw_ref, db_ref, lng_ref, lnb_ref, wcp_ref, bcp_ref, wrp_ref, wo_ref, nfg_ref, wr_ref, br_ref,
                x1_ref, h2_ref, ridx_ref, rgate_ref, uext, cscr):
    ts = cv_ref.shape[0]
    i = pl.program_id(1)
    last = pl.num_programs(1) - 1
    up = jnp.where(i > 0, cvp_ref[...] * _sigmoid(cgp_ref[...]), 0.0)
    un = jnp.where(i < last, cvn_ref[...] * _sigmoid(cgn_ref[...]), 0.0)
    um = cv_ref[...] * _sigmoid(cg_ref[...])
    for c in range(D_MODEL // LANES):
        lanes = slice(c * LANES, (c + 1) * LANES)
        uext[c, 0:MIX_HALO, :] = up[:, lanes]
        uext[c, MIX_HALO:MIX_HALO + ts, :] = um[:, lanes]
        uext[c, MIX_HALO + ts:2 * MIX_HALO + ts, :] = un[:, lanes]

    off = MIX_HALO - CONV_WIDTH // 2

    def conv_chunk(rc, carry):
        r0 = pl.multiple_of(rc * MIX_RC, MIX_RC)
        for c in range(D_MODEL // LANES):
            lanes = slice(c * LANES, (c + 1) * LANES)
            acc = db_ref[:, lanes] + dw_ref[0:1, lanes] * uext[c, pl.ds(r0 + off, MIX_RC), :]
            for k in range(1, CONV_WIDTH):
                acc = acc + dw_ref[k:k + 1, lanes] * uext[c, pl.ds(r0 + (off + k), MIX_RC), :]
            cscr[pl.ds(r0, MIX_RC), lanes] = acc
        return carry

    lax.fori_loop(0, ts // MIX_RC, conv_chunk, 0)

    c = cscr[...]
    mu = jnp.mean(c, axis=-1, keepdims=True)
    cc = c - mu
    var = jnp.mean(cc * cc, axis=-1, keepdims=True)
    y = cc * lax.rsqrt(var + EPS) * lng_ref[...] + lnb_ref[...]
    sw = y * _sigmoid(y)
    conv = jnp.dot(sw.astype(BF16), wcp_ref[...], preferred_element_type=F32) + bcp_ref[...]
    rnn = jnp.dot(g_ref[...], wrp_ref[...], preferred_element_type=F32)
    merged = _sigmoid(gr_ref[...]) * rnn + _sigmoid(gc_ref[...]) * conv
    x1 = x_ref[...] + jnp.dot(merged.astype(BF16), wo_ref[...], preferred_element_type=F32)
    x1_ref[...] = x1
    h2 = _rms(x1, nfg_ref[...])
    h2_ref[...] = h2

    logits = jnp.dot(h2.astype(BF16), wr_ref[...], preferred_element_type=F32) + br_ref[...]
    lane = lax.broadcasted_iota(jnp.int32, logits.shape, 1)
    vals, idxs = [], []
    for _ in range(TOP_K):
        m = jnp.max(logits, axis=-1, keepdims=True)
        idx = jnp.min(jnp.where(logits == m, lane, LANES), axis=-1, keepdims=True)
        vals.append(m)
        idxs.append(idx)
        logits = jnp.where(lane == idx, 2.0 * NEG_BIG, logits)
    es = [jnp.exp(v - vals[0]) for v in vals]
    inv = 1.0 / (es[0] + es[1] + es[2] + es[3])
    ridx = jnp.zeros(logits.shape, jnp.int32)
    rgate = jnp.zeros(logits.shape, F32)
    for k in range(TOP_K):
        ridx = jnp.where(lane == k, idxs[k], ridx)
        rgate = jnp.where(lane == k, es[k] * inv, rgate)
    ridx_ref[...] = ridx
    rgate_ref[...] = rgate


def _mix(z3, g3, x3, dw, db, lng, lnb, wcp, bcp, wrp, wo, nfg, wr, br):
    b, s, _ = z3.shape
    ts = MIX_TS
    nt = s // ts
    hb = ts // MIX_HALO
    nhb = s // MIX_HALO
    tile = lambda col: pl.BlockSpec((None, ts, D_MODEL), lambda i, j, col=col: (i, j, col))
    prev = lambda col: pl.BlockSpec((None, MIX_HALO, D_MODEL),
                                    lambda i, j, col=col: (i, jnp.maximum(j * hb - 1, 0), col))
    nxt = lambda col: pl.BlockSpec((None, MIX_HALO, D_MODEL),
                                   lambda i, j, col=col: (i, jnp.minimum((j + 1) * hb, nhb - 1), col))
    full = lambda shape: pl.BlockSpec(shape, lambda i, j: (0,) * len(shape))
    out_tile = lambda w: pl.BlockSpec((None, ts, w), lambda i, j: (i, j, 0))
    return pl.pallas_call(
        _mix_kernel,
        grid=(b, nt),
        in_specs=[
            tile(2), tile(3), prev(2), prev(3), nxt(2), nxt(3),
            tile(0),
            tile(4), tile(5),
            tile(0),
            full((CONV_WIDTH, D_MODEL)), full((1, D_MODEL)), full((1, D_MODEL)), full((1, D_MODEL)),
            full((D_MODEL, D_MODEL)), full((1, D_MODEL)), full((D_MODEL, D_MODEL)), full((D_MODEL, D_MODEL)),
            full((1, D_MODEL)), full((D_MODEL, LANES)), full((1, LANES)),
        ],
        out_specs=[out_tile(D_MODEL), out_tile(D_MODEL), out_tile(LANES), out_tile(LANES)],
        out_shape=[
            jax.ShapeDtypeStruct((b, s, D_MODEL), F32),
            jax.ShapeDtypeStruct((b, s, D_MODEL), F32),
            jax.ShapeDtypeStruct((b, s, LANES), jnp.int32),
            jax.ShapeDtypeStruct((b, s, LANES), F32),
        ],
        scratch_shapes=[pltpu.VMEM((D_MODEL // LANES, ts + 2 * MIX_HALO, LANES), F32),
                        pltpu.VMEM((ts, D_MODEL), F32)],
        compiler_params=_cparams(("arbitrary", "arbitrary")),
        name="mix",
    )(z3, z3, z3, z3, z3, z3, g3, z3, z3, x3, dw, db, lng, lnb, wcp, bcp, wrp, wo, nfg, wr, br)


MOE_BM = 256


def _moe_kernel(be_ref, bn_ref, tok_ref, tokn_ref, dst_ref, h_hbm, wgu_ref, bgu_ref, wd_ref, bd_ref, out_hbm,
                xbuf, ybuf, gsem, ssem, *, n_tok):
    del be_ref
    i = pl.program_id(0)
    nb = pl.num_programs(0)
    slot = i % 2

    def round8(n):
        return pl.multiple_of((n + (SUBLANES - 1)) & (-SUBLANES), SUBLANES)

    def gather_row(tok, r, s):
        return pltpu.make_async_copy(h_hbm.at[pl.ds(tok[0, 0, r], 1)], xbuf.at[s, pl.ds(r, 1)], gsem.at[s])

    def scatter_row(r, s):
        return pltpu.make_async_copy(ybuf.at[s, pl.ds(r, 1)], out_hbm.at[pl.ds(dst_ref[0, 0, r], 1)], ssem.at[s])

    def issue_rows(make, n):
        def body(it, c):
            base = pl.multiple_of(it * SUBLANES, SUBLANES)
            for u in range(SUBLANES):
                make(base + u).start()
            return c
        lax.fori_loop(0, lax.shift_right_logical(n + (SUBLANES - 1), 3), body, 0)

    def issue_gather(tok, n, s):
        issue_rows(lambda r: gather_row(tok, r, s), n)

    def wait_gather(n, s):
        n8 = round8(n)
        pltpu.make_async_copy(h_hbm.at[pl.ds(0, n8)], xbuf.at[s, pl.ds(0, n8)], gsem.at[s]).wait()

    def wait_scatter(n, s):
        n8 = round8(n)
        pltpu.make_async_copy(ybuf.at[s, pl.ds(0, n8)], out_hbm.at[pl.ds(0, n8)], ssem.at[s]).wait()

    @pl.when(i == 0)
    def _():
        xbuf[...] = jnp.zeros(xbuf.shape, xbuf.dtype)
        init_dump = pltpu.make_async_copy(xbuf.at[0, pl.ds(0, 2 * SUBLANES)],
                                          out_hbm.at[pl.ds(TOP_K * n_tok, 2 * SUBLANES)], ssem.at[0])
        init_dump.start()
        init_dump.wait()
        issue_gather(tok_ref, bn_ref[0], 0)

    @pl.when(i + 1 < nb)
    def _():
        issue_gather(tokn_ref, bn_ref[i + 1], 1 - slot)

    n_old = bn_ref[jnp.maximum(i - 2, 0)]

    @pl.when(jnp.logical_and(i >= 2, n_old > 0))
    def _():
        wait_scatter(n_old, slot)

    n = bn_ref[i]

    @pl.when(n > 0)
    def _():
        wait_gather(n, slot)
        xb = xbuf[slot].astype(BF16)
        gu = jnp.dot(xb, wgu_ref[...], preferred_element_type=F32) + bgu_ref[...]
        gt = jnp.minimum(gu[:, :D_EXPERT], SWIGLU_LIMIT)
        upv = jnp.clip(gu[:, D_EXPERT:], -SWIGLU_LIMIT, SWIGLU_LIMIT)
        act = (upv + 1.0) * (gt * _sigmoid(gt * SWIGLU_ALPHA))
        ybuf[slot] = jnp.dot(act.astype(BF16), wd_ref[...], preferred_element_type=F32) + bd_ref[...]
        issue_rows(lambda r: scatter_row(r, slot), n)

    @pl.when(i == nb - 1)
    def _():
        n_prev = bn_ref[jnp.maximum(i - 1, 0)]

        @pl.when(jnp.logical_and(i >= 1, n_prev > 0))
        def _():
            wait_scatter(n_prev, 1 - slot)

        @pl.when(n > 0)
        def _():
            wait_scatter(n, slot)


def _moe(h2, block_e, block_n, row_tok, row_dst, wgu, bgu, wd, bd):
    t = h2.shape[0]
    nb = block_e.shape[0]
    bm = MOE_BM
    grid_spec = pltpu.PrefetchScalarGridSpec(
        num_scalar_prefetch=2,
        grid=(nb,),
        in_specs=[
            pl.BlockSpec((1, 1, bm), lambda i, be, bn: (i, 0, 0), memory_space=pltpu.SMEM),
            pl.BlockSpec((1, 1, bm), lambda i, be, bn: (jnp.minimum(i + 1, nb - 1), 0, 0),
                         memory_space=pltpu.SMEM),
            pl.BlockSpec((1, 1, bm), lambda i, be, bn: (i, 0, 0), memory_space=pltpu.SMEM),
            pl.BlockSpec(memory_space=pl.ANY),
            pl.BlockSpec((None, D_MODEL, 2 * D_EXPERT), lambda i, be, bn: (be[i], 0, 0)),
            pl.BlockSpec((None, 1, 2 * D_EXPERT), lambda i, be, bn: (be[i], 0, 0)),
            pl.BlockSpec((None, D_EXPERT, D_MODEL), lambda i, be, bn: (be[i], 0, 0)),
            pl.BlockSpec((None, 1, D_MODEL), lambda i, be, bn: (be[i], 0, 0)),
        ],
        out_specs=pl.BlockSpec(memory_space=pl.ANY),
        scratch_shapes=[
            pltpu.VMEM((2, bm, D_MODEL), F32),
            pltpu.VMEM((2, bm, D_MODEL), F32),
            pltpu.SemaphoreType.DMA((2,)),
            pltpu.SemaphoreType.DMA((2,)),
        ],
    )
    return pl.pallas_call(
        functools.partial(_moe_kernel, n_tok=t),
        grid_spec=grid_spec,
        out_shape=jax.ShapeDtypeStruct((TOP_K * t + 2 * SUBLANES, D_MODEL), F32),
        compiler_params=_cparams(("arbitrary",)),
        name="moe",
    )(block_e, block_n, row_tok, row_tok, row_dst, h2, wgu, bgu, wd, bd)


def _route(ridx, bm):
    t = ridx.shape[0]
    tk = t * TOP_K
    flat_e = ridx[:, :TOP_K].reshape(-1)
    order = jnp.argsort(flat_e, stable=True).astype(jnp.int32)
    experts = jnp.arange(N_EXPERTS, dtype=jnp.int32)
    counts = jnp.sum((flat_e[:, None] == experts[None, :]).astype(jnp.int32), axis=0)
    nblk = (counts + bm - 1) // bm
    blk_end = jnp.cumsum(nblk)
    blk_start = blk_end - nblk
    start = jnp.cumsum(counts) - counts
    n_blocks = tk // bm + N_EXPERTS
    b = jnp.arange(n_blocks, dtype=jnp.int32)
    block_e = jnp.minimum(jnp.sum((b[:, None] >= blk_end[None, :]).astype(jnp.int32), axis=1), N_EXPERTS - 1)
    sel = (block_e[:, None] == experts[None, :]).astype(jnp.int32)
    pick = lambda v: jnp.sum(sel * v[None, :], axis=1)
    within = (b - pick(blk_start)) * bm
    block_n = jnp.clip(pick(counts) - within, 0, bm).astype(jnp.int32)
    r = jnp.arange(bm, dtype=jnp.int32)
    j = (pick(start) + within)[:, None] + r[None, :]
    valid = r[None, :] < block_n[:, None]
    src = order[jnp.clip(j, 0, tk - 1)]
    tok = lax.shift_right_logical(src, 2)
    dump = TOP_K * t + (b & 1)[:, None] * SUBLANES + (r & (SUBLANES - 1))[None, :]
    row_tok = jnp.where(valid, tok, 0).astype(jnp.int32)
    row_dst = jnp.where(valid, (src & (TOP_K - 1)) * t + tok, dump).astype(jnp.int32)
    return (block_e.astype(jnp.int32), block_n, row_tok.reshape(n_blocks, 1, bm),
            row_dst.reshape(n_blocks, 1, bm))


FIN_TM = 512


def _final_kernel(x1_ref, y0_ref, y1_ref, y2_ref, y3_ref, rg_ref, p_ref, npg_ref, wpg_ref, wpp_ref, ppg_ref,
                  fg_ref, o_ref):
    rg = rg_ref[...]
    x2 = x1_ref[...]
    for k, y_ref in enumerate((y0_ref, y1_ref, y2_ref, y3_ref)):
        x2 = x2 + rg[:, k:k + 1] * y_ref[...]
    h = _rms(x2, npg_ref[...])
    gate = _sigmoid(jnp.dot(h.astype(BF16), wpg_ref[...], preferred_element_type=F32))
    e = _rms(jnp.dot(p_ref[...].astype(BF16), wpp_ref[...], preferred_element_type=F32), ppg_ref[...])
    x3 = x2 + gate * e
    o_ref[...] = _rms(x3, fg_ref[...])


def _final(x1, ys, rgate, p, npg, wpg, wpp, ppg, fg):
    t = x1.shape[0]
    tm = FIN_TM
    nt = t // tm
    full = lambda shape: pl.BlockSpec(shape, lambda i: (0,) * len(shape))
    slab = lambda k: pl.BlockSpec((tm, D_MODEL), lambda i, k=k: (k * nt + i, 0))
    return pl.pallas_call(
        _final_kernel,
        grid=(nt,),
        in_specs=[
            pl.BlockSpec((tm, D_MODEL), lambda i: (i, 0)),
            slab(0), slab(1), slab(2), slab(3),
            pl.BlockSpec((tm, LANES), lambda i: (i, 0)),
            pl.BlockSpec((tm, D_PLE), lambda i: (i, 0)),
            full((1, D_MODEL)), full((D_MODEL, D_MODEL)), full((D_PLE, D_MODEL)), full((1, D_MODEL)),
            full((1, D_MODEL)),
        ],
        out_specs=pl.BlockSpec((tm, D_MODEL), lambda i: (i, 0)),
        out_shape=jax.ShapeDtypeStruct((t, D_MODEL), F32),
        compiler_params=_cparams(("arbitrary",)),
        name="final",
    )(x1, ys, ys, ys, ys, rgate, p, npg, wpg, wpp, ppg, fg)


def _block_diag(w):
    per = RNN_CB // RNN_HEAD_DIM
    groups = D_MODEL // RNN_CB
    w = w.reshape(2, groups, per, RNN_HEAD_DIM, RNN_HEAD_DIM)
    eye = jnp.eye(per, dtype=w.dtype)
    bd = jnp.einsum("dgpij,pq->dgpiqj", w, eye)
    return bd.reshape(2, groups, RNN_CB, RNN_CB)


def _trunk(x, p, wts):
    b, s, d = x.shape
    t = b * s
    row = lambda v: v.reshape(1, -1)
    z = _inproj(x.reshape(t, d), row(wts["norm_mix_g"]), wts["w_in"], row(wts["b_in"]))
    z3 = z.reshape(b, s, D_IN)
    g3 = _rnn(z3, wts["conv_rnn_w"], row(wts["conv_rnn_b"]), wts["w_a_bd"], wts["b_a"], wts["w_x_bd"], wts["b_x"],
              wts["rg_c"])
    x1, h2, ridx, rgate = _mix(z3, g3, x, wts["conv_dw_w"], row(wts["conv_dw_b"]), row(wts["ln_g"]),
                               row(wts["ln_b"]), wts["w_conv_proj"], row(wts["b_conv_proj"]), wts["w_rnn_proj"],
                               wts["w_out"], row(wts["norm_ffn_g"]), wts["w_router"], wts["b_router"])
    ridx = ridx.reshape(t, LANES)
    block_e, block_n, row_tok, row_dst = _route(ridx, MOE_BM)
    ys = _moe(h2.reshape(t, d), block_e, block_n, row_tok, row_dst, wts["w_gu"], wts["b_gu"], wts["w_down"],
              wts["b_down"])
    y = _final(x1.reshape(t, d), ys, rgate.reshape(t, LANES), p.reshape(t, D_PLE), row(wts["norm_ple_g"]),
               wts["w_ple_gate"], wts["w_ple_proj"], row(wts["ple_post_g"]), row(wts["final_g"]))
    return y.reshape(b, s, d)


def kernel(x_prompt, x_sample, p_prompt, p_sample, norm_mix_g, w_in, b_in, conv_rnn_w, conv_rnn_b, w_a, b_a, w_x,
           b_x, lam, w_rnn_proj, conv_dw_w, conv_dw_b, ln_g, ln_b, w_conv_proj, b_conv_proj, w_out, norm_ffn_g,
           w_router, b_router, w_gu, b_gu, w_down, b_down, norm_ple_g, w_ple_gate, w_ple_proj, ple_post_g, final_g):
    l = 0
    wts = {
        "norm_mix_g": norm_mix_g[l], "w_in": w_in[l].astype(BF16), "b_in": b_in[l],
        "conv_rnn_w": conv_rnn_w[l], "conv_rnn_b": conv_rnn_b[l],
        "w_a_bd": _block_diag(w_a[l]).astype(BF16), "b_a": b_a[l],
        "w_x_bd": _block_diag(w_x[l]).astype(BF16), "b_x": b_x[l],
        "rg_c": RG_C * jax.nn.softplus(-lam[l]),
        "w_rnn_proj": w_rnn_proj[l].astype(BF16),
        "conv_dw_w": conv_dw_w[l], "conv_dw_b": conv_dw_b[l], "ln_g": ln_g[l], "ln_b": ln_b[l],
        "w_conv_proj": w_conv_proj[l].astype(BF16), "b_conv_proj": b_conv_proj[l],
        "w_out": w_out[l].astype(BF16), "norm_ffn_g": norm_ffn_g[l],
        "w_router": jnp.pad(w_router[l], ((0, 0), (0, LANES - N_EXPERTS))).astype(BF16),
        "b_router": jnp.pad(b_router[l], (0, LANES - N_EXPERTS), constant_values=NEG_BIG).reshape(1, LANES),
        "w_gu": w_gu[l].astype(BF16), "b_gu": b_gu[l].reshape(N_EXPERTS, 1, 2 * D_EXPERT),
        "w_down": w_down[l].astype(BF16), "b_down": b_down[l].reshape(N_EXPERTS, 1, D_MODEL),
        "norm_ple_g": norm_ple_g[l], "w_ple_gate": w_ple_gate[l].astype(BF16),
        "w_ple_proj": w_ple_proj[l].astype(BF16), "ple_post_g": ple_post_g[l], "final_g": final_g,
    }
    y_prompt = _trunk(x_prompt, p_prompt[l], wts)
    y_sample = _trunk(x_sample, p_sample[l], wts)
    return (y_prompt, y_sample)
```

```python
import functools

import jax
import jax.numpy as jnp
from jax import lax
from jax.experimental import pallas as pl
from jax.experimental.pallas import tpu as pltpu

D_MODEL = 1024
D_IN = 6 * D_MODEL
RNN_HEADS = 16
RNN_HEAD_DIM = D_MODEL // RNN_HEADS
RNN_CONV = 4
RG_C = 8.0
CONV_WIDTH = 31
N_EXPERTS = 32
TOP_K = 4
D_EXPERT = D_MODEL
SWIGLU_LIMIT = 7.0
SWIGLU_ALPHA = 1.702
D_PLE = 256
EPS = 1e-6

LANES = 128
SUBLANES = 8
NEG_BIG = -1e30

F32 = jnp.float32
BF16 = jnp.bfloat16

VMEM_LIMIT = 56 * 1024 * 1024


def _cparams(sem):
    return pltpu.CompilerParams(dimension_semantics=sem, vmem_limit_bytes=VMEM_LIMIT)


def _rms(x, g):
    return x * lax.rsqrt(jnp.mean(x * x, axis=-1, keepdims=True) + EPS) * g


def _sigmoid(x):
    return 0.5 * jnp.tanh(0.5 * x) + 0.5


def _inproj_kernel(x_ref, g_ref, w_ref, b_ref, z_ref, h_scr):
    @pl.when(pl.program_id(1) == 0)
    def _():
        h_scr[...] = _rms(x_ref[...], g_ref[...]).astype(BF16)

    z_ref[...] = jnp.dot(h_scr[...], w_ref[...], preferred_element_type=F32) + b_ref[...]


def _inproj(x, g, w, b, tm=1024, tn=2048):
    t = x.shape[0]
    return pl.pallas_call(
        _inproj_kernel,
        grid=(t // tm, D_IN // tn),
        in_specs=[
            pl.BlockSpec((tm, D_MODEL), lambda i, j: (i, 0)),
            pl.BlockSpec((1, D_MODEL), lambda i, j: (0, 0)),
            pl.BlockSpec((D_MODEL, tn), lambda i, j: (0, j)),
            pl.BlockSpec((1, tn), lambda i, j: (0, j)),
        ],
        out_specs=pl.BlockSpec((tm, tn), lambda i, j: (i, j)),
        out_shape=jax.ShapeDtypeStruct((t, D_IN), F32),
        scratch_shapes=[pltpu.VMEM((tm, D_MODEL), BF16)],
        compiler_params=_cparams(("arbitrary", "arbitrary")),
        name="inproj",
    )(x, g, w, b)


RNN_CB = 256
RNN_CHUNK = 128


def _rnn_kernel(xr_ref, yr_ref, cw_ref, cb_ref, wa_ref, ba_ref, wx_ref, bx_ref, c_ref, o_ref, xpad, hf):
    s, cb = xr_ref.shape
    chunk = RNN_CHUNK
    nchunks = s // chunk
    zeros8 = jnp.zeros((SUBLANES, LANES), F32)
    for c in range(cb // LANES):
        xpad[c, 0:SUBLANES, :] = zeros8
        xpad[c, s + SUBLANES:s + 2 * SUBLANES, :] = zeros8
        xpad[c, SUBLANES:s + SUBLANES, :] = xr_ref[:, c * LANES:(c + 1) * LANES]
    row8 = lax.broadcasted_iota(jnp.int32, (SUBLANES, cb), 0)

    def conv(t0):
        parts = []
        for c in range(cb // LANES):
            lanes = slice(c * LANES, (c + 1) * LANES)
            acc = cb_ref[:, lanes] + cw_ref[0:1, lanes] * xpad[c, pl.ds(t0 + 6, chunk), :]
            for k in range(1, RNN_CONV):
                acc = acc + cw_ref[k:k + 1, lanes] * xpad[c, pl.ds(t0 + (6 + k), chunk), :]
            parts.append(acc)
        return jnp.concatenate(parts, axis=1)

    def gates(xc, d):
        xb = xc.astype(BF16)
        r = _sigmoid(jnp.dot(xb, wa_ref[d], preferred_element_type=F32) + ba_ref[d:d + 1, :])
        ig = _sigmoid(jnp.dot(xb, wx_ref[d], preferred_element_type=F32) + bx_ref[d:d + 1, :])
        a = jnp.exp(-c_ref[d:d + 1, :] * r)
        u = jnp.sqrt(1.0 - a * a) * (ig * xc)
        return a, u

    def tile_scan(a, u, reverse):
        for dist in (1, 2, 4):
            if reverse:
                keep = row8 < SUBLANES - dist
                shift = SUBLANES - dist
            else:
                keep = row8 >= dist
                shift = dist
            us = jnp.where(keep, pltpu.roll(u, shift, 0), 0.0)
            a_s = jnp.where(keep, pltpu.roll(a, shift, 0), 1.0)
            u = u + a * us
            a = a * a_s
        return a, u

    def fwd_body(i, h):
        t0 = pl.multiple_of(i * chunk, chunk)
        a, u = gates(conv(t0), 0)
        tiles = []
        for j in range(chunk // SUBLANES):
            at, ut = tile_scan(a[j * 8:(j + 1) * 8], u[j * 8:(j + 1) * 8], False)
            ht = ut + at * h
            h = ht[7:8, :]
            tiles.append(ht)
        hf[pl.ds(t0, chunk), :] = jnp.concatenate(tiles, axis=0)
        return h

    lax.fori_loop(0, nchunks, fwd_body, jnp.zeros((1, cb), F32))

    def bwd_body(i, h):
        t0 = pl.multiple_of((nchunks - 1 - i) * chunk, chunk)
        a, u = gates(conv(t0), 1)
        tiles = [None] * (chunk // SUBLANES)
        for j in reversed(range(chunk // SUBLANES)):
            at, ut = tile_scan(a[j * 8:(j + 1) * 8], u[j * 8:(j + 1) * 8], True)
            ht = ut + at * h
            h = ht[0:1, :]
            tiles[j] = ht
        hb = jnp.concatenate(tiles, axis=0)
        y = yr_ref[pl.ds(t0, chunk), :]
        o_ref[pl.ds(t0, chunk), :] = ((hf[pl.ds(t0, chunk), :] + hb) * jax.nn.gelu(y)).astype(o_ref.dtype)
        return h

    lax.fori_loop(0, nchunks, bwd_body, jnp.zeros((1, cb), F32))


def _rnn(z3, cw, cb_, wa_bd, ba, wx_bd, bx, c):
    b, s, _ = z3.shape
    nc = D_MODEL // RNN_CB
    return pl.pallas_call(
        _rnn_kernel,
        grid=(b, nc),
        in_specs=[
            pl.BlockSpec((None, s, RNN_CB), lambda i, j: (i, 0, j)),
            pl.BlockSpec((None, s, RNN_CB), lambda i, j: (i, 0, nc + j)),
            pl.BlockSpec((RNN_CONV, RNN_CB), lambda i, j: (0, j)),
            pl.BlockSpec((1, RNN_CB), lambda i, j: (0, j)),
            pl.BlockSpec((2, None, RNN_CB, RNN_CB), lambda i, j: (0, j, 0, 0)),
            pl.BlockSpec((2, RNN_CB), lambda i, j: (0, j)),
            pl.BlockSpec((2, None, RNN_CB, RNN_CB), lambda i, j: (0, j, 0, 0)),
            pl.BlockSpec((2, RNN_CB), lambda i, j: (0, j)),
            pl.BlockSpec((2, RNN_CB), lambda i, j: (0, j)),
        ],
        out_specs=pl.BlockSpec((None, s, RNN_CB), lambda i, j: (i, 0, j)),
        out_shape=jax.ShapeDtypeStruct((b, s, D_MODEL), BF16),
        scratch_shapes=[pltpu.VMEM((RNN_CB // LANES, s + 2 * SUBLANES, LANES), F32),
                        pltpu.VMEM((s, RNN_CB), F32)],
        compiler_params=_cparams(("arbitrary", "arbitrary")),
        name="rnn",
    )(z3, z3, cw, cb_, wa_bd, ba, wx_bd, bx, c)


MIX_TS = 512
MIX_HALO = 16
MIX_RC = 64


def _mix_kernel(cv_ref, cg_ref, cvp_ref, cgp_ref, cvn_ref, cgn_ref, g_ref, gr_ref, gc_ref, x_ref,
                dw_ref, db_ref, lng_ref, lnb_ref, wcp_ref, bcp_ref, wrp_ref, wo_ref, nfg_ref, wr_ref, br_ref,
                x1_ref, h2_ref, ridx_ref, rgate_ref, uext, cscr):
    ts = cv_ref.shape[0]
    i = pl.program_id(1)
    last = pl.num_programs(1) - 1
    up = jnp.where(i > 0, cvp_ref[...] * _sigmoid(cgp_ref[...]), 0.0)
    un = jnp.where(i < last, cvn_ref[...] * _sigmoid(cgn_ref[...]), 0.0)
    um = cv_ref[...] * _sigmoid(cg_ref[...])
    for c in range(D_MODEL // LANES):
        lanes = slice(c * LANES, (c + 1) * LANES)
        uext[c, 0:MIX_HALO, :] = up[:, lanes]
        uext[c, MIX_HALO:MIX_HALO + ts, :] = um[:, lanes]
        uext[c, MIX_HALO + ts:2 * MIX_HALO + ts, :] = un[:, lanes]

    off = MIX_HALO - CONV_WIDTH // 2

    def conv_chunk(rc, carry):
        r0 = pl.multiple_of(rc * MIX_RC, MIX_RC)
        for c in range(D_MODEL // LANES):
            lanes = slice(c * LANES, (c + 1) * LANES)
            acc = db_ref[:, lanes] + dw_ref[0:1, lanes] * uext[c, pl.ds(r0 + off, MIX_RC), :]
            for k in range(1, CONV_WIDTH):
                acc = acc + dw_ref[k:k + 1, lanes] * uext[c, pl.ds(r0 + (off + k), MIX_RC), :]
            cscr[pl.ds(r0, MIX_RC), lanes] = acc
        return carry

    lax.fori_loop(0, ts // MIX_RC, conv_chunk, 0)

    c = cscr[...]
    mu = jnp.mean(c, axis=-1, keepdims=True)
    cc = c - mu
    var = jnp.mean(cc * cc, axis=-1, keepdims=True)
    y = cc * lax.rsqrt(var + EPS) * lng_ref[...] + lnb_ref[...]
    sw = y * _sigmoid(y)
    conv = jnp.dot(sw.astype(BF16), wcp_ref[...], preferred_element_type=F32) + bcp_ref[...]
    rnn = jnp.dot(g_ref[...], wrp_ref[...], preferred_element_type=F32)
    merged = _sigmoid(gr_ref[...]) * rnn + _sigmoid(gc_ref[...]) * conv
    x1 = x_ref[...] + jnp.dot(merged.astype(BF16), wo_ref[...], preferred_element_type=F32)
    x1_ref[...] = x1
    h2 = _rms(x1, nfg_ref[...])
    h2_ref[...] = h2

    logits = jnp.dot(h2.astype(BF16), wr_ref[...], preferred_element_type=F32) + br_ref[...]
    lane = lax.broadcasted_iota(jnp.int32, logits.shape, 1)
    vals, idxs = [], []
    for _ in range(TOP_K):
        m = jnp.max(logits, axis=-1, keepdims=True)
        idx = jnp.min(jnp.where(logits == m, lane, LANES), axis=-1, keepdims=True)
        vals.append(m)
        idxs.append(idx)
        logits = jnp.where(lane == idx, 2.0 * NEG_BIG, logits)
    es = [jnp.exp(v - vals[0]) for v in vals]
    inv = 1.0 / (es[0] + es[1] + es[2] + es[3])
    ridx = jnp.zeros(logits.shape, jnp.int32)
    rgate = jnp.zeros(logits.shape, F32)
    for k in range(TOP_K):
        ridx = jnp.where(lane == k, idxs[k], ridx)
        rgate = jnp.where(lane == k, es[k] * inv, rgate)
    ridx_ref[...] = ridx
    rgate_ref[...] = rgate


def _mix(z3, g3, x3, dw, db, lng, lnb, wcp, bcp, wrp, wo, nfg, wr, br):
    b, s, _ = z3.shape
    ts = MIX_TS
    nt = s // ts
    hb = ts // MIX_HALO
    nhb = s // MIX_HALO
    tile = lambda col: pl.BlockSpec((None, ts, D_MODEL), lambda i, j, col=col: (i, j, col))
    prev = lambda col: pl.BlockSpec((None, MIX_HALO, D_MODEL),
                                    lambda i, j, col=col: (i, jnp.maximum(j * hb - 1, 0), col))
    nxt = lambda col: pl.BlockSpec((None, MIX_HALO, D_MODEL),
                                   lambda i, j, col=col: (i, jnp.minimum((j + 1) * hb, nhb - 1), col))
    full = lambda shape: pl.BlockSpec(shape, lambda i, j: (0,) * len(shape))
    out_tile = lambda w: pl.BlockSpec((None, ts, w), lambda i, j: (i, j, 0))
    return pl.pallas_call(
        _mix_kernel,
        grid=(b, nt),
        in_specs=[
            tile(2), tile(3), prev(2), prev(3), nxt(2), nxt(3),
            tile(0),
            tile(4), tile(5),
            tile(0),
            full((CONV_WIDTH, D_MODEL)), full((1, D_MODEL)), full((1, D_MODEL)), full((1, D_MODEL)),
            full((D_MODEL, D_MODEL)), full((1, D_MODEL)), full((D_MODEL, D_MODEL)), full((D_MODEL, D_MODEL)),
            full((1, D_MODEL)), full((D_MODEL, LANES)), full((1, LANES)),
        ],
        out_specs=[out_tile(D_MODEL), out_tile(D_MODEL), out_tile(LANES), out_tile(LANES)],
        out_shape=[
            jax.ShapeDtypeStruct((b, s, D_MODEL), F32),
            jax.ShapeDtypeStruct((b, s, D_MODEL), F32),
            jax.ShapeDtypeStruct((b, s, LANES), jnp.int32),
            jax.ShapeDtypeStruct((b, s, LANES), F32),
        ],
        scratch_shapes=[pltpu.VMEM((D_MODEL // LANES, ts + 2 * MIX_HALO, LANES), F32),
                        pltpu.VMEM((ts, D_MODEL), F32)],
        compiler_params=_cparams(("arbitrary", "arbitrary")),
        name="mix",
    )(z3, z3, z3, z3, z3, z3, g3, z3, z3, x3, dw, db, lng, lnb, wcp, bcp, wrp, wo, nfg, wr, br)


MOE_BM = 256


def _moe_kernel(be_ref, tok_ref, tokn_ref, dst_ref, dstp_ref, h_hbm,
                wgu0_ref, bgu0_ref, wd0_ref, bd0_ref, wgu1_ref, bgu1_ref, wd1_ref, bd1_ref, out_hbm,
                xbuf, ybuf, gsem, ssem, *, n_tok):
    del be_ref
    bm = xbuf.shape[1]
    i = pl.program_id(0)
    last = pl.num_programs(0) - 1

    def start_gather(tok, j, s):
        for r in range(bm):
            pltpu.make_async_copy(h_hbm.at[pl.ds(tok[j, 0, r], 1)], xbuf.at[s, pl.ds(r, 1)], gsem.at[s]).start()

    def start_scatter(dst, j, s):
        for r in range(bm):
            pltpu.make_async_copy(ybuf.at[s, pl.ds(r, 1)], out_hbm.at[pl.ds(dst[j, 0, r], 1)], ssem.at[s]).start()

    def wait_gather(s):
        pltpu.make_async_copy(h_hbm.at[pl.ds(0, bm)], xbuf.at[s], gsem.at[s]).wait()

    def wait_scatter(s):
        pltpu.make_async_copy(ybuf.at[s], out_hbm.at[pl.ds(0, bm)], ssem.at[s]).wait()

    def block(s, next_tok, wgu_ref, bgu_ref, wd_ref, bd_ref):
        wait_gather(s)
        wait_scatter(s)
        start_gather(*next_tok, 1 - s)
        start_scatter(dstp_ref, s, 1 - s)
        xb = xbuf[s].astype(BF16)
        gu = jnp.dot(xb, wgu_ref[...], preferred_element_type=F32) + bgu_ref[...]
        gt = jnp.minimum(gu[:, :D_EXPERT], SWIGLU_LIMIT)
        upv = jnp.clip(gu[:, D_EXPERT:], -SWIGLU_LIMIT, SWIGLU_LIMIT)
        act = (upv + 1.0) * (gt * _sigmoid(gt * SWIGLU_ALPHA))
        ybuf[s] = jnp.dot(act.astype(BF16), wd_ref[...], preferred_element_type=F32) + bd_ref[...]

    @pl.when(i == 0)
    def _():
        ybuf[...] = jnp.zeros(ybuf.shape, ybuf.dtype)
        pltpu.make_async_copy(ybuf.at[0], out_hbm.at[pl.ds(TOP_K * n_tok, bm)], ssem.at[0]).start()
        start_gather(tok_ref, 0, 0)

    block(0, (tok_ref, 1), wgu0_ref, bgu0_ref, wd0_ref, bd0_ref)
    block(1, (tokn_ref, 0), wgu1_ref, bgu1_ref, wd1_ref, bd1_ref)

    @pl.when(i == last)
    def _():
        wait_gather(0)
        wait_scatter(0)
        start_scatter(dst_ref, 1, 1)
        wait_scatter(1)


def _moe(h2, block_e, row_tok, row_dst, row_dstp, wgu, bgu, wd, bd):
    t = h2.shape[0]
    nb = block_e.shape[0]
    bm = MOE_BM
    npairs = nb // 2
    rows = lambda index: pl.BlockSpec((2, 1, bm), index, memory_space=pltpu.SMEM)
    wspec = lambda shape, j: pl.BlockSpec((None,) + shape, lambda i, be, j=j: (be[2 * i + j], 0, 0))
    weights = lambda j: [wspec((D_MODEL, 2 * D_EXPERT), j), wspec((1, 2 * D_EXPERT), j),
                         wspec((D_EXPERT, D_MODEL), j), wspec((1, D_MODEL), j)]
    grid_spec = pltpu.PrefetchScalarGridSpec(
        num_scalar_prefetch=1,
        grid=(npairs,),
        in_specs=[
            rows(lambda i, be: (i, 0, 0)),
            rows(lambda i, be: (jnp.minimum(i + 1, npairs - 1), 0, 0)),
            rows(lambda i, be: (i, 0, 0)),
            rows(lambda i, be: (i, 0, 0)),
            pl.BlockSpec(memory_space=pl.ANY),
        ] + weights(0) + weights(1),
        out_specs=pl.BlockSpec(memory_space=pl.ANY),
        scratch_shapes=[
            pltpu.VMEM((2, bm, D_MODEL), F32),
            pltpu.VMEM((2, bm, D_MODEL), F32),
            pltpu.SemaphoreType.DMA((2,)),
            pltpu.SemaphoreType.DMA((2,)),
        ],
    )
    return pl.pallas_call(
        functools.partial(_moe_kernel, n_tok=t),
        grid_spec=grid_spec,
        out_shape=jax.ShapeDtypeStruct((TOP_K * t + 2 * bm, D_MODEL), F32),
        compiler_params=_cparams(("arbitrary",)),
        name="moe",
    )(block_e, row_tok, row_tok, row_dst, row_dstp, h2, wgu, bgu, wd, bd, wgu, bgu, wd, bd)


def _route(ridx, bm):
    t = ridx.shape[0]
    tk = t * TOP_K
    flat_e = ridx[:, :TOP_K].reshape(-1)
    order = jnp.argsort(flat_e, stable=True).astype(jnp.int32)
    experts = jnp.arange(N_EXPERTS, dtype=jnp.int32)
    counts = jnp.sum((flat_e[:, None] == experts[None, :]).astype(jnp.int32), axis=0)
    nblk = (counts + bm - 1) // bm
    blk_end = jnp.cumsum(nblk)
    blk_start = blk_end - nblk
    start = jnp.cumsum(counts) - counts
    n_blocks = tk // bm + N_EXPERTS
    b = jnp.arange(n_blocks, dtype=jnp.int32)
    block_e = jnp.minimum(jnp.sum((b[:, None] >= blk_end[None, :]).astype(jnp.int32), axis=1), N_EXPERTS - 1)
    sel = (block_e[:, None] == experts[None, :]).astype(jnp.int32)
    pick = lambda v: jnp.sum(sel * v[None, :], axis=1)
    within = (b - pick(blk_start)) * bm
    block_n = jnp.clip(pick(counts) - within, 0, bm).astype(jnp.int32)
    r = jnp.arange(bm, dtype=jnp.int32)
    j = (pick(start) + within)[:, None] + r[None, :]
    valid = r[None, :] < block_n[:, None]
    src = order[jnp.clip(j, 0, tk - 1)]
    tok = lax.shift_right_logical(src, 2)
    dump = TOP_K * t + (b & 1)[:, None] * bm + r[None, :]
    row_tok = jnp.where(valid, tok, 0).astype(jnp.int32)
    row_dst = jnp.where(valid, (src & (TOP_K - 1)) * t + tok, dump).astype(jnp.int32)
    first_prev = (TOP_K * t + bm + r)[None, :]
    row_dstp = jnp.concatenate([first_prev, row_dst[:-1]], axis=0)
    shape = (n_blocks, 1, bm)
    return block_e.astype(jnp.int32), row_tok.reshape(shape), row_dst.reshape(shape), row_dstp.reshape(shape)


FIN_TM = 512


def _final_kernel(x1_ref, y0_ref, y1_ref, y2_ref, y3_ref, rg_ref, p_ref, npg_ref, wpg_ref, wpp_ref, ppg_ref,
                  fg_ref, o_ref):
    rg = rg_ref[...]
    x2 = x1_ref[...]
    for k, y_ref in enumerate((y0_ref, y1_ref, y2_ref, y3_ref)):
        x2 = x2 + rg[:, k:k + 1] * y_ref[...]
    h = _rms(x2, npg_ref[...])
    gate = _sigmoid(jnp.dot(h.astype(BF16), wpg_ref[...], preferred_element_type=F32))
    e = _rms(jnp.dot(p_ref[...].astype(BF16), wpp_ref[...], preferred_element_type=F32), ppg_ref[...])
    x3 = x2 + gate * e
    o_ref[...] = _rms(x3, fg_ref[...])


def _final(x1, ys, rgate, p, npg, wpg, wpp, ppg, fg):
    t = x1.shape[0]
    tm = FIN_TM
    nt = t // tm
    full = lambda shape: pl.BlockSpec(shape, lambda i: (0,) * len(shape))
    slab = lambda k: pl.BlockSpec((tm, D_MODEL), lambda i, k=k: (k * nt + i, 0))
    return pl.pallas_call(
        _final_kernel,
        grid=(nt,),
        in_specs=[
            pl.BlockSpec((tm, D_MODEL), lambda i: (i, 0)),
            slab(0), slab(1), slab(2), slab(3),
            pl.BlockSpec((tm, LANES), lambda i: (i, 0)),
            pl.BlockSpec((tm, D_PLE), lambda i: (i, 0)),
            full((1, D_MODEL)), full((D_MODEL, D_MODEL)), full((D_PLE, D_MODEL)), full((1, D_MODEL)),
            full((1, D_MODEL)),
        ],
        out_specs=pl.BlockSpec((tm, D_MODEL), lambda i: (i, 0)),
        out_shape=jax.ShapeDtypeStruct((t, D_MODEL), F32),
        compiler_params=_cparams(("arbitrary",)),
        name="final",
    )(x1, ys, ys, ys, ys, rgate, p, npg, wpg, wpp, ppg, fg)


def _block_diag(w):
    per = RNN_CB // RNN_HEAD_DIM
    groups = D_MODEL // RNN_CB
    w = w.reshape(2, groups, per, RNN_HEAD_DIM, RNN_HEAD_DIM)
    eye = jnp.eye(per, dtype=w.dtype)
    bd = jnp.einsum("dgpij,pq->dgpiqj", w, eye)
    return bd.reshape(2, groups, RNN_CB, RNN_CB)


def _trunk(x, p, wts):
    b, s, d = x.shape
    t = b * s
    row = lambda v: v.reshape(1, -1)
    z = _inproj(x.reshape(t, d), row(wts["norm_mix_g"]), wts["w_in"], row(wts["b_in"]))
    z3 = z.reshape(b, s, D_IN)
    g3 = _rnn(z3, wts["conv_rnn_w"], row(wts["conv_rnn_b"]), wts["w_a_bd"], wts["b_a"], wts["w_x_bd"], wts["b_x"],
              wts["rg_c"])
    x1, h2, ridx, rgate = _mix(z3, g3, x, wts["conv_dw_w"], row(wts["conv_dw_b"]), row(wts["ln_g"]),
                               row(wts["ln_b"]), wts["w_conv_proj"], row(wts["b_conv_proj"]), wts["w_rnn_proj"],
                               wts["w_out"], row(wts["norm_ffn_g"]), wts["w_router"], wts["b_router"])
    ridx = ridx.reshape(t, LANES)
    block_e, row_tok, row_dst, row_dstp = _route(ridx, MOE_BM)
    ys = _moe(h2.reshape(t, d), block_e, row_tok, row_dst, row_dstp, wts["w_gu"], wts["b_gu"], wts["w_down"],
              wts["b_down"])
    y = _final(x1.reshape(t, d), ys, rgate.reshape(t, LANES), p.reshape(t, D_PLE), row(wts["norm_ple_g"]),
               wts["w_ple_gate"], wts["w_ple_proj"], row(wts["ple_post_g"]), row(wts["final_g"]))
    return y.reshape(b, s, d)


def kernel(x_prompt, x_sample, p_prompt, p_sample, norm_mix_g, w_in, b_in, conv_rnn_w, conv_rnn_b, w_a, b_a, w_x,
           b_x, lam, w_rnn_proj, conv_dw_w, conv_dw_b, ln_g, ln_b, w_conv_proj, b_conv_proj, w_out, norm_ffn_g,
           w_router, b_router, w_gu, b_gu, w_down, b_down, norm_ple_g, w_ple_gate, w_ple_proj, ple_post_g, final_g):
    l = 0
    wts = {
        "norm_mix_g": norm_mix_g[l], "w_in": w_in[l].astype(BF16), "b_in": b_in[l],
        "conv_rnn_w": conv_rnn_w[l], "conv_rnn_b": conv_rnn_b[l],
        "w_a_bd": _block_diag(w_a[l]).astype(BF16), "b_a": b_a[l],
        "w_x_bd": _block_diag(w_x[l]).astype(BF16), "b_x": b_x[l],
        "rg_c": RG_C * jax.nn.softplus(-lam[l]),
        "w_rnn_proj": w_rnn_proj[l].astype(BF16),
        "conv_dw_w": conv_dw_w[l], "conv_dw_b": conv_dw_b[l], "ln_g": ln_g[l], "ln_b": ln_b[l],
        "w_conv_proj": w_conv_proj[l].astype(BF16), "b_conv_proj": b_conv_proj[l],
        "w_out": w_out[l].astype(BF16), "norm_ffn_g": norm_ffn_g[l],
        "w_router": jnp.pad(w_router[l], ((0, 0), (0, LANES - N_EXPERTS))).astype(BF16),
        "b_router": jnp.pad(b_router[l], (0, LANES - N_EXPERTS), constant_values=NEG_BIG).reshape(1, LANES),
        "w_gu": w_gu[l].astype(BF16), "b_gu": b_gu[l].reshape(N_EXPERTS, 1, 2 * D_EXPERT),
        "w_down": w_down[l].astype(BF16), "b_down": b_down[l].reshape(N_EXPERTS, 1, D_MODEL),
        "norm_ple_g": norm_ple_g[l], "w_ple_gate": w_ple_gate[l].astype(BF16),
        "w_ple_proj": w_ple_proj[l].astype(BF16), "ple_post_g": ple_post_g[l], "final_g": final_g,
    }
    y_prompt = _trunk(x_prompt, p_prompt[l], wts)
    y_sample = _trunk(x_sample, p_sample[l], wts)
    return (y_prompt, y_sample)
```

```python
import jax
import jax.numpy as jnp
from jax import lax
from jax.experimental import pallas as pl
from jax.experimental.pallas import tpu as pltpu

D_MODEL = 1024
D_IN = 6 * D_MODEL
RNN_HEADS = 16
RNN_HEAD_DIM = D_MODEL // RNN_HEADS
RNN_CONV = 4
RG_C = 8.0
CONV_WIDTH = 31
N_EXPERTS = 32
TOP_K = 4
D_EXPERT = D_MODEL
SWIGLU_LIMIT = 7.0
SWIGLU_ALPHA = 1.702
D_PLE = 256
EPS = 1e-6

LANES = 128
SUBLANES = 8
NEG_BIG = -1e30

F32 = jnp.float32
BF16 = jnp.bfloat16

VMEM_LIMIT = 56 * 1024 * 1024


def _cparams(sem):
    return pltpu.CompilerParams(dimension_semantics=sem, vmem_limit_bytes=VMEM_LIMIT)


def _rms(x, g):
    return x * lax.rsqrt(jnp.mean(x * x, axis=-1, keepdims=True) + EPS) * g


def _sigmoid(x):
    return 0.5 * jnp.tanh(0.5 * x) + 0.5


def _inproj_kernel(x_ref, g_ref, w_ref, b_ref, z_ref, h_scr):
    @pl.when(pl.program_id(1) == 0)
    def _():
        h_scr[...] = _rms(x_ref[...], g_ref[...]).astype(BF16)

    z_ref[...] = jnp.dot(h_scr[...], w_ref[...], preferred_element_type=F32) + b_ref[...]


def _inproj(x, g, w, b, tm=1024, tn=2048):
    t = x.shape[0]
    return pl.pallas_call(
        _inproj_kernel,
        grid=(t // tm, D_IN // tn),
        in_specs=[
            pl.BlockSpec((tm, D_MODEL), lambda i, j: (i, 0)),
            pl.BlockSpec((1, D_MODEL), lambda i, j: (0, 0)),
            pl.BlockSpec((D_MODEL, tn), lambda i, j: (0, j)),
            pl.BlockSpec((1, tn), lambda i, j: (0, j)),
        ],
        out_specs=pl.BlockSpec((tm, tn), lambda i, j: (i, j)),
        out_shape=jax.ShapeDtypeStruct((t, D_IN), F32),
        scratch_shapes=[pltpu.VMEM((tm, D_MODEL), BF16)],
        compiler_params=_cparams(("arbitrary", "arbitrary")),
        name="inproj",
    )(x, g, w, b)


RNN_CB = 256
RNN_CHUNK = 128


def _rnn_kernel(xr_ref, yr_ref, cw_ref, cb_ref, wa_ref, ba_ref, wx_ref, bx_ref, c_ref, o_ref, xpad, hf):
    s, cb = xr_ref.shape
    chunk = RNN_CHUNK
    nchunks = s // chunk
    zeros8 = jnp.zeros((SUBLANES, LANES), F32)
    for c in range(cb // LANES):
        xpad[c, 0:SUBLANES, :] = zeros8
        xpad[c, s + SUBLANES:s + 2 * SUBLANES, :] = zeros8
        xpad[c, SUBLANES:s + SUBLANES, :] = xr_ref[:, c * LANES:(c + 1) * LANES]
    row8 = lax.broadcasted_iota(jnp.int32, (SUBLANES, cb), 0)

    def conv(t0):
        parts = []
        for c in range(cb // LANES):
            lanes = slice(c * LANES, (c + 1) * LANES)
            acc = cb_ref[:, lanes] + cw_ref[0:1, lanes] * xpad[c, pl.ds(t0 + 6, chunk), :]
            for k in range(1, RNN_CONV):
                acc = acc + cw_ref[k:k + 1, lanes] * xpad[c, pl.ds(t0 + (6 + k), chunk), :]
            parts.append(acc)
        return jnp.concatenate(parts, axis=1)

    def gates(xc, d):
        xb = xc.astype(BF16)
        r = _sigmoid(jnp.dot(xb, wa_ref[d], preferred_element_type=F32) + ba_ref[d:d + 1, :])
        ig = _sigmoid(jnp.dot(xb, wx_ref[d], preferred_element_type=F32) + bx_ref[d:d + 1, :])
        a = jnp.exp(-c_ref[d:d + 1, :] * r)
        u = jnp.sqrt(1.0 - a * a) * (ig * xc)
        return a, u

    def tile_scan(a, u, reverse):
        for dist in (1, 2, 4):
            if reverse:
                keep = row8 < SUBLANES - dist
                shift = SUBLANES - dist
            else:
                keep = row8 >= dist
                shift = dist
            us = jnp.where(keep, pltpu.roll(u, shift, 0), 0.0)
            a_s = jnp.where(keep, pltpu.roll(a, shift, 0), 1.0)
            u = u + a * us
            a = a * a_s
        return a, u

    def fwd_body(i, h):
        t0 = pl.multiple_of(i * chunk, chunk)
        a, u = gates(conv(t0), 0)
        tiles = []
        for j in range(chunk // SUBLANES):
            at, ut = tile_scan(a[j * 8:(j + 1) * 8], u[j * 8:(j + 1) * 8], False)
            ht = ut + at * h
            h = ht[7:8, :]
            tiles.append(ht)
        hf[pl.ds(t0, chunk), :] = jnp.concatenate(tiles, axis=0)
        return h

    lax.fori_loop(0, nchunks, fwd_body, jnp.zeros((1, cb), F32))

    def bwd_body(i, h):
        t0 = pl.multiple_of((nchunks - 1 - i) * chunk, chunk)
        a, u = gates(conv(t0), 1)
        tiles = [None] * (chunk // SUBLANES)
        for j in reversed(range(chunk // SUBLANES)):
            at, ut = tile_scan(a[j * 8:(j + 1) * 8], u[j * 8:(j + 1) * 8], True)
            ht = ut + at * h
            h = ht[0:1, :]
            tiles[j] = ht
        hb = jnp.concatenate(tiles, axis=0)
        y = yr_ref[pl.ds(t0, chunk), :]
        o_ref[pl.ds(t0, chunk), :] = ((hf[pl.ds(t0, chunk), :] + hb) * jax.nn.gelu(y)).astype(o_ref.dtype)
        return h

    lax.fori_loop(0, nchunks, bwd_body, jnp.zeros((1, cb), F32))


def _rnn(z3, cw, cb_, wa_bd, ba, wx_bd, bx, c):
    b, s, _ = z3.shape
    nc = D_MODEL // RNN_CB
    return pl.pallas_call(
        _rnn_kernel,
        grid=(b, nc),
        in_specs=[
            pl.BlockSpec((None, s, RNN_CB), lambda i, j: (i, 0, j)),
            pl.BlockSpec((None, s, RNN_CB), lambda i, j: (i, 0, nc + j)),
            pl.BlockSpec((RNN_CONV, RNN_CB), lambda i, j: (0, j)),
            pl.BlockSpec((1, RNN_CB), lambda i, j: (0, j)),
            pl.BlockSpec((2, None, RNN_CB, RNN_CB), lambda i, j: (0, j, 0, 0)),
            pl.BlockSpec((2, RNN_CB), lambda i, j: (0, j)),
            pl.BlockSpec((2, None, RNN_CB, RNN_CB), lambda i, j: (0, j, 0, 0)),
            pl.BlockSpec((2, RNN_CB), lambda i, j: (0, j)),
            pl.BlockSpec((2, RNN_CB), lambda i, j: (0, j)),
        ],
        out_specs=pl.BlockSpec((None, s, RNN_CB), lambda i, j: (i, 0, j)),
        out_shape=jax.ShapeDtypeStruct((b, s, D_MODEL), BF16),
        scratch_shapes=[pltpu.VMEM((RNN_CB // LANES, s + 2 * SUBLANES, LANES), F32),
                        pltpu.VMEM((s, RNN_CB), F32)],
        compiler_params=_cparams(("arbitrary", "arbitrary")),
        name="rnn",
    )(z3, z3, cw, cb_, wa_bd, ba, wx_bd, bx, c)


MIX_TS = 512
MIX_HALO = 16
MIX_RC = 64


def _mix_kernel(cv_ref, cg_ref, cvp_ref, cgp_ref, cvn_ref, cgn_ref, g_ref, gr_ref, gc_ref, x_ref,
                dw_ref, db_ref, lng_ref, lnb_ref, wcp_ref, bcp_ref, wrp_ref, wo_ref, nfg_ref, wr_ref, br_ref,
                x1_ref, h2_ref, ridx_ref, rgate_ref, uext, cscr):
    ts = cv_ref.shape[0]
    i = pl.program_id(1)
    last = pl.num_programs(1) - 1
    up = jnp.where(i > 0, cvp_ref[...] * _sigmoid(cgp_ref[...]), 0.0)
    un = jnp.where(i < last, cvn_ref[...] * _sigmoid(cgn_ref[...]), 0.0)
    um = cv_ref[...] * _sigmoid(cg_ref[...])
    for c in range(D_MODEL // LANES):
        lanes = slice(c * LANES, (c + 1) * LANES)
        uext[c, 0:MIX_HALO, :] = up[:, lanes]
        uext[c, MIX_HALO:MIX_HALO + ts, :] = um[:, lanes]
        uext[c, MIX_HALO + ts:2 * MIX_HALO + ts, :] = un[:, lanes]

    off = MIX_HALO - CONV_WIDTH // 2

    def conv_chunk(rc, carry):
        r0 = pl.multiple_of(rc * MIX_RC, MIX_RC)
        for c in range(D_MODEL // LANES):
            lanes = slice(c * LANES, (c + 1) * LANES)
            acc = db_ref[:, lanes] + dw_ref[0:1, lanes] * uext[c, pl.ds(r0 + off, MIX_RC), :]
            for k in range(1, CONV_WIDTH):
                acc = acc + dw_ref[k:k + 1, lanes] * uext[c, pl.ds(r0 + (off + k), MIX_RC), :]
            cscr[pl.ds(r0, MIX_RC), lanes] = acc
        return carry

    lax.fori_loop(0, ts // MIX_RC, conv_chunk, 0)

    c = cscr[...]
    mu = jnp.mean(c, axis=-1, keepdims=True)
    cc = c - mu
    var = jnp.mean(cc * cc, axis=-1, keepdims=True)
    y = cc * lax.rsqrt(var + EPS) * lng_ref[...] + lnb_ref[...]
    sw = y * _sigmoid(y)
    conv = jnp.dot(sw.astype(BF16), wcp_ref[...], preferred_element_type=F32) + bcp_ref[...]
    rnn = jnp.dot(g_ref[...], wrp_ref[...], preferred_element_type=F32)
    merged = _sigmoid(gr_ref[...]) * rnn + _sigmoid(gc_ref[...]) * conv
    x1 = x_ref[...] + jnp.dot(merged.astype(BF16), wo_ref[...], preferred_element_type=F32)
    x1_ref[...] = x1
    h2 = _rms(x1, nfg_ref[...])
    h2_ref[...] = h2

    logits = jnp.dot(h2.astype(BF16), wr_ref[...], preferred_element_type=F32) + br_ref[...]
    lane = lax.broadcasted_iota(jnp.int32, logits.shape, 1)
    vals, idxs = [], []
    for _ in range(TOP_K):
        m = jnp.max(logits, axis=-1, keepdims=True)
        idx = jnp.min(jnp.where(logits == m, lane, LANES), axis=-1, keepdims=True)
        vals.append(m)
        idxs.append(idx)
        logits = jnp.where(lane == idx, 2.0 * NEG_BIG, logits)
    es = [jnp.exp(v - vals[0]) for v in vals]
    inv = 1.0 / (es[0] + es[1] + es[2] + es[3])
    ridx = jnp.zeros(logits.shape, jnp.int32)
    rgate = jnp.zeros(logits.shape, F32)
    for k in range(TOP_K):
        ridx = jnp.where(lane == k, idxs[k], ridx)
        rgate = jnp.where(lane == k, es[k] * inv, rgate)
    ridx_ref[...] = ridx
    rgate_ref[...] = rgate


def _mix(z3, g3, x3, dw, db, lng, lnb, wcp, bcp, wrp, wo, nfg, wr, br):
    b, s, _ = z3.shape
    ts = MIX_TS
    nt = s // ts
    hb = ts // MIX_HALO
    nhb = s // MIX_HALO
    tile = lambda col: pl.BlockSpec((None, ts, D_MODEL), lambda i, j, col=col: (i, j, col))
    prev = lambda col: pl.BlockSpec((None, MIX_HALO, D_MODEL),
                                    lambda i, j, col=col: (i, jnp.maximum(j * hb - 1, 0), col))
    nxt = lambda col: pl.BlockSpec((None, MIX_HALO, D_MODEL),
                                   lambda i, j, col=col: (i, jnp.minimum((j + 1) * hb, nhb - 1), col))
    full = lambda shape: pl.BlockSpec(shape, lambda i, j: (0,) * len(shape))
    out_tile = lambda w: pl.BlockSpec((None, ts, w), lambda i, j: (i, j, 0))
    return pl.pallas_call(
        _mix_kernel,
        grid=(b, nt),
        in_specs=[
            tile(2), tile(3), prev(2), prev(3), nxt(2), nxt(3),
            tile(0),
            tile(4), tile(5),
            tile(0),
            full((CONV_WIDTH, D_MODEL)), full((1, D_MODEL)), full((1, D_MODEL)), full((1, D_MODEL)),
            full((D_MODEL, D_MODEL)), full((1, D_MODEL)), full((D_MODEL, D_MODEL)), full((D_MODEL, D_MODEL)),
            full((1, D_MODEL)), full((D_MODEL, LANES)), full((1, LANES)),
        ],
        out_specs=[out_tile(D_MODEL), out_tile(D_MODEL), out_tile(LANES), out_tile(LANES)],
        out_shape=[
            jax.ShapeDtypeStruct((b, s, D_MODEL), F32),
            jax.ShapeDtypeStruct((b, s, D_MODEL), F32),
            jax.ShapeDtypeStruct((b, s, LANES), jnp.int32),
            jax.ShapeDtypeStruct((b, s, LANES), F32),
        ],
        scratch_shapes=[pltpu.VMEM((D_MODEL // LANES, ts + 2 * MIX_HALO, LANES), F32),
                        pltpu.VMEM((ts, D_MODEL), F32)],
        compiler_params=_cparams(("arbitrary", "arbitrary")),
        name="mix",
    )(z3, z3, z3, z3, z3, z3, g3, z3, z3, x3, dw, db, lng, lnb, wcp, bcp, wrp, wo, nfg, wr, br)


MOE_BM = 256
MOE_TT = 256
MOE_PER = TOP_K * MOE_TT
CHUNKS = D_MODEL // LANES


def _to_flat(flat_ref, lead, x):
    rows = x.shape[0]
    for c in range(CHUNKS):
        flat_ref[lead + (pl.ds(c, rows, stride=CHUNKS), slice(None))] = x[:, c * LANES:(c + 1) * LANES]


def _from_flat(flat_ref, lead, rows):
    return jnp.concatenate(
        [flat_ref[lead + (pl.ds(c, rows, stride=CHUNKS), slice(None))] for c in range(CHUNKS)], axis=1)


def _run_copies(cnt_ref, off_ref, glob_ref, tile, make):
    for e in range(N_EXPERTS):
        k = tile * N_EXPERTS + e
        n = cnt_ref[k]

        @pl.when(n > 0)
        def _(k=k, n=n):
            make(pl.multiple_of(off_ref[k] * CHUNKS, CHUNKS), pl.multiple_of(glob_ref[k] * CHUNKS, CHUNKS),
                 pl.multiple_of(n * CHUNKS, CHUNKS)).start()


def _disp_kernel(cnt_ref, off_ref, glob_ref, zst_ref, zn_ref, ord_ref, h_ref, xs_hbm, hflat, cbuf, zbuf, sem, zsem):
    i = pl.program_id(0)
    last = pl.num_programs(0) - 1
    slot = i % 2

    def whole(s):
        return pltpu.make_async_copy(cbuf.at[s], xs_hbm.at[pl.ds(0, MOE_PER * CHUNKS)], sem.at[s])

    @pl.when(i >= 2)
    def _():
        whole(slot).wait()

    _to_flat(hflat, (), h_ref[...])

    def compact(it, carry):
        for u in range(SUBLANES):
            j = it * SUBLANES + u
            src = pl.multiple_of(ord_ref[0, 0, j] * CHUNKS, CHUNKS)
            cbuf[slot, pl.ds(pl.multiple_of(j * CHUNKS, CHUNKS), CHUNKS), :] = hflat[pl.ds(src, CHUNKS), :]
        return carry

    lax.fori_loop(0, MOE_PER // SUBLANES, compact, 0)

    _run_copies(cnt_ref, off_ref, glob_ref, i, lambda loc, glob, n: pltpu.make_async_copy(
        cbuf.at[slot, pl.ds(loc, n)], xs_hbm.at[pl.ds(glob, n)], sem.at[slot]))

    @pl.when(i == last)
    def _():
        zbuf[...] = jnp.zeros(zbuf.shape, zbuf.dtype)
        for e in range(N_EXPERTS):
            n = zn_ref[e]

            @pl.when(n > 0)
            def _(e=e, n=n):
                rows = pl.multiple_of(n * CHUNKS, CHUNKS)
                cp = pltpu.make_async_copy(zbuf.at[pl.ds(0, rows)],
                                           xs_hbm.at[pl.ds(pl.multiple_of(zst_ref[e] * CHUNKS, CHUNKS), rows)], zsem)
                cp.start()
                cp.wait()

        def zero_block(k, carry):
            row0 = pl.multiple_of((zst_ref[N_EXPERTS] + k * MOE_BM) * CHUNKS, CHUNKS)
            cp = pltpu.make_async_copy(zbuf, xs_hbm.at[pl.ds(row0, MOE_BM * CHUNKS)], zsem)
            cp.start()
            cp.wait()
            return carry

        lax.fori_loop(0, zn_ref[N_EXPERTS], zero_block, 0)

        @pl.when(i >= 1)
        def _():
            whole(1 - slot).wait()

        whole(slot).wait()


def _dispatch(h2, tables, ord_local, n_rows):
    t = h2.shape[0]
    nt = t // MOE_TT
    cnt, off, glob, zst, zn = tables
    grid_spec = pltpu.PrefetchScalarGridSpec(
        num_scalar_prefetch=5,
        grid=(nt,),
        in_specs=[
            pl.BlockSpec((1, 1, MOE_PER), lambda i, *_: (i, 0, 0), memory_space=pltpu.SMEM),
            pl.BlockSpec((MOE_TT, D_MODEL), lambda i, *_: (i, 0)),
        ],
        out_specs=pl.BlockSpec(memory_space=pl.ANY),
        scratch_shapes=[
            pltpu.VMEM((MOE_TT * CHUNKS, LANES), F32),
            pltpu.VMEM((2, MOE_PER * CHUNKS, LANES), F32),
            pltpu.VMEM((MOE_BM * CHUNKS, LANES), F32),
            pltpu.SemaphoreType.DMA((2,)),
            pltpu.SemaphoreType.DMA(()),
        ],
    )
    return pl.pallas_call(
        _disp_kernel,
        grid_spec=grid_spec,
        out_shape=jax.ShapeDtypeStruct((n_rows * CHUNKS, LANES), F32),
        compiler_params=_cparams(("arbitrary",)),
        name="dispatch",
    )(cnt, off, glob, zst, zn, ord_local, h2)


def _experts_kernel(be_ref, nused_ref, xs_ref, wgu_ref, bgu_ref, wd_ref, bd_ref, ys_ref):
    del be_ref
    bm = MOE_BM
    b = pl.program_id(0)

    @pl.when(b < nused_ref[0])
    def _():
        xb = _from_flat(xs_ref, (), bm).astype(BF16)
        gu = jnp.dot(xb, wgu_ref[...], preferred_element_type=F32) + bgu_ref[...]
        gt = jnp.minimum(gu[:, :D_EXPERT], SWIGLU_LIMIT)
        upv = jnp.clip(gu[:, D_EXPERT:], -SWIGLU_LIMIT, SWIGLU_LIMIT)
        act = (upv + 1.0) * (gt * _sigmoid(gt * SWIGLU_ALPHA))
        y = jnp.dot(act.astype(BF16), wd_ref[...], preferred_element_type=F32) + bd_ref[...]
        _to_flat(ys_ref, (), y)

    @pl.when(b >= nused_ref[0])
    def _():
        ys_ref[...] = jnp.zeros(ys_ref.shape, ys_ref.dtype)


def _experts(xs, block_e, n_used, wgu, bgu, wd, bd):
    nb = block_e.shape[0]
    bm = MOE_BM
    wspec = lambda shape: pl.BlockSpec((None,) + shape, lambda i, be, nu: (be[i], 0, 0))
    grid_spec = pltpu.PrefetchScalarGridSpec(
        num_scalar_prefetch=2,
        grid=(nb,),
        in_specs=[
            pl.BlockSpec((bm * CHUNKS, LANES), lambda i, be, nu: (i, 0)),
            wspec((D_MODEL, 2 * D_EXPERT)), wspec((1, 2 * D_EXPERT)), wspec((D_EXPERT, D_MODEL)), wspec((1, D_MODEL)),
        ],
        out_specs=pl.BlockSpec((bm * CHUNKS, LANES), lambda i, be, nu: (i, 0)),
    )
    return pl.pallas_call(
        _experts_kernel,
        grid_spec=grid_spec,
        out_shape=jax.ShapeDtypeStruct(xs.shape, F32),
        compiler_params=_cparams(("arbitrary",)),
        name="experts",
    )(block_e, n_used, xs, wgu, bgu, wd, bd)


def _combine_kernel(cnt_ref, off_ref, glob_ref, x1_ref, tok_ref, gate_ref, p_ref, npg_ref, wpg_ref, wpp_ref, ppg_ref,
                    fg_ref, ys_hbm, o_ref, gbuf, sem):
    i = pl.program_id(0)
    nt = pl.num_programs(0)
    slot = i % 2

    def fetch(tile, s):
        _run_copies(cnt_ref, off_ref, glob_ref, tile, lambda loc, glob, n: pltpu.make_async_copy(
            ys_hbm.at[pl.ds(glob, n)], gbuf.at[s, pl.ds(loc, n)], sem.at[s]))

    @pl.when(i == 0)
    def _():
        fetch(0, 0)

    @pl.when(i + 1 < nt)
    def _():
        fetch(i + 1, 1 - slot)

    pltpu.make_async_copy(ys_hbm.at[pl.ds(0, MOE_PER * CHUNKS)], gbuf.at[slot], sem.at[slot]).wait()

    y = _from_flat(gbuf, (slot,), MOE_PER).astype(BF16)
    tok_of_row = tok_ref[0]
    hit = lax.broadcasted_iota(jnp.int32, (MOE_TT, MOE_PER), 0) == tok_of_row
    sel = jnp.where(hit, gate_ref[0], 0.0).astype(BF16)
    x2 = x1_ref[...] + jnp.dot(sel, y, preferred_element_type=F32)
    h = _rms(x2, npg_ref[...])
    gate = _sigmoid(jnp.dot(h.astype(BF16), wpg_ref[...], preferred_element_type=F32))
    emb = _rms(jnp.dot(p_ref[...].astype(BF16), wpp_ref[...], preferred_element_type=F32), ppg_ref[...])
    x3 = x2 + gate * emb
    o_ref[...] = _rms(x3, fg_ref[...])


def _combine(x1, ys, tables, tok_of_row, gate_of_row, p, npg, wpg, wpp, ppg, fg):
    t = x1.shape[0]
    tm = MOE_TT
    nt = t // tm
    cnt, off, glob = tables
    full = lambda shape: pl.BlockSpec(shape, lambda i, *_: (0,) * len(shape))
    grid_spec = pltpu.PrefetchScalarGridSpec(
        num_scalar_prefetch=3,
        grid=(nt,),
        in_specs=[
            pl.BlockSpec((tm, D_MODEL), lambda i, *_: (i, 0)),
            pl.BlockSpec((1, 1, MOE_PER), lambda i, *_: (i, 0, 0)),
            pl.BlockSpec((1, 1, MOE_PER), lambda i, *_: (i, 0, 0)),
            pl.BlockSpec((tm, D_PLE), lambda i, *_: (i, 0)),
            full((1, D_MODEL)), full((D_MODEL, D_MODEL)), full((D_PLE, D_MODEL)), full((1, D_MODEL)),
            full((1, D_MODEL)),
            pl.BlockSpec(memory_space=pl.ANY),
        ],
        out_specs=pl.BlockSpec((tm, D_MODEL), lambda i, *_: (i, 0)),
        scratch_shapes=[pltpu.VMEM((2, MOE_PER * CHUNKS, LANES), F32), pltpu.SemaphoreType.DMA((2,))],
    )
    return pl.pallas_call(
        _combine_kernel,
        grid_spec=grid_spec,
        out_shape=jax.ShapeDtypeStruct((t, D_MODEL), F32),
        compiler_params=_cparams(("arbitrary",)),
        name="combine",
    )(cnt, off, glob, x1, tok_of_row, gate_of_row, p, npg, wpg, wpp, ppg, fg, ys)


def _route_tiles(ridx, rgate, bm):
    t = ridx.shape[0]
    tk = t * TOP_K
    nt = t // MOE_TT
    flat_e = ridx[:, :TOP_K].reshape(-1)
    flat_g = rgate[:, :TOP_K].reshape(-1)
    pos = jnp.arange(tk, dtype=jnp.int32)
    tile_of = pos // MOE_PER
    order = jnp.argsort(tile_of * N_EXPERTS + flat_e, stable=True).astype(jnp.int32)
    tok_local = (lax.shift_right_logical(order, 2) - tile_of * MOE_TT).astype(jnp.int32)
    gate_of_row = flat_g[order]
    experts = jnp.arange(N_EXPERTS, dtype=jnp.int32)
    cnt = jnp.sum((flat_e.reshape(nt, MOE_PER, 1) == experts[None, None, :]).astype(jnp.int32), axis=1)
    off = jnp.cumsum(cnt, axis=1) - cnt
    counts = jnp.sum(cnt, axis=0)
    nblk = (counts + bm - 1) // bm
    blk_end = jnp.cumsum(nblk)
    pad_start = (blk_end - nblk) * bm
    glob = pad_start[None, :] + jnp.cumsum(cnt, axis=0) - cnt
    n_blocks = tk // bm + N_EXPERTS
    b = jnp.arange(n_blocks, dtype=jnp.int32)
    block_e = jnp.minimum(jnp.sum((b[:, None] >= blk_end[None, :]).astype(jnp.int32), axis=1), N_EXPERTS - 1)
    flat = lambda v: v.reshape(-1).astype(jnp.int32)
    tables = (flat(cnt), flat(off), flat(glob))
    zero_runs = (flat(jnp.concatenate([pad_start + counts, blk_end[-1:] * bm])),
                 flat(jnp.concatenate([nblk * bm - counts, n_blocks - blk_end[-1:]])))
    shape = (nt, 1, MOE_PER)
    return (tables, zero_runs, tok_local.reshape(shape), gate_of_row.reshape(shape), block_e.astype(jnp.int32),
            blk_end[-1:].astype(jnp.int32), n_blocks * bm)


def _block_diag(w):
    per = RNN_CB // RNN_HEAD_DIM
    groups = D_MODEL // RNN_CB
    w = w.reshape(2, groups, per, RNN_HEAD_DIM, RNN_HEAD_DIM)
    eye = jnp.eye(per, dtype=w.dtype)
    bd = jnp.einsum("dgpij,pq->dgpiqj", w, eye)
    return bd.reshape(2, groups, RNN_CB, RNN_CB)


def _trunk(x, p, wts):
    b, s, d = x.shape
    t = b * s
    row = lambda v: v.reshape(1, -1)
    z = _inproj(x.reshape(t, d), row(wts["norm_mix_g"]), wts["w_in"], row(wts["b_in"]))
    z3 = z.reshape(b, s, D_IN)
    g3 = _rnn(z3, wts["conv_rnn_w"], row(wts["conv_rnn_b"]), wts["w_a_bd"], wts["b_a"], wts["w_x_bd"], wts["b_x"],
              wts["rg_c"])
    x1, h2, ridx, rgate = _mix(z3, g3, x, wts["conv_dw_w"], row(wts["conv_dw_b"]), row(wts["ln_g"]),
                               row(wts["ln_b"]), wts["w_conv_proj"], row(wts["b_conv_proj"]), wts["w_rnn_proj"],
                               wts["w_out"], row(wts["norm_ffn_g"]), wts["w_router"], wts["b_router"])
    tables, zero_runs, tok_of_row, gate_of_row, block_e, n_used, n_rows = _route_tiles(
        ridx.reshape(t, LANES), rgate.reshape(t, LANES), MOE_BM)
    xs = _dispatch(h2.reshape(t, d), tables + zero_runs, tok_of_row, n_rows)
    ys = _experts(xs, block_e, n_used, wts["w_gu"], wts["b_gu"], wts["w_down"], wts["b_down"])
    y = _combine(x1.reshape(t, d), ys, tables, tok_of_row, gate_of_row, p.reshape(t, D_PLE),
                 row(wts["norm_ple_g"]), wts["w_ple_gate"], wts["w_ple_proj"], row(wts["ple_post_g"]),
                 row(wts["final_g"]))
    return y.reshape(b, s, d)


def kernel(x_prompt, x_sample, p_prompt, p_sample, norm_mix_g, w_in, b_in, conv_rnn_w, conv_rnn_b, w_a, b_a, w_x,
           b_x, lam, w_rnn_proj, conv_dw_w, conv_dw_b, ln_g, ln_b, w_conv_proj, b_conv_proj, w_out, norm_ffn_g,
           w_router, b_router, w_gu, b_gu, w_down, b_down, norm_ple_g, w_ple_gate, w_ple_proj, ple_post_g, final_g):
    l = 0
    wts = {
        "norm_mix_g": norm_mix_g[l], "w_in": w_in[l].astype(BF16), "b_in": b_in[l],
        "conv_rnn_w": conv_rnn_w[l], "conv_rnn_b": conv_rnn_b[l],
        "w_a_bd": _block_diag(w_a[l]).astype(BF16), "b_a": b_a[l],
        "w_x_bd": _block_diag(w_x[l]).astype(BF16), "b_x": b_x[l],
        "rg_c": RG_C * jax.nn.softplus(-lam[l]),
        "w_rnn_proj": w_rnn_proj[l].astype(BF16),
        "conv_dw_w": conv_dw_w[l], "conv_dw_b": conv_dw_b[l], "ln_g": ln_g[l], "ln_b": ln_b[l],
        "w_conv_proj": w_conv_proj[l].astype(BF16), "b_conv_proj": b_conv_proj[l],
        "w_out": w_out[l].astype(BF16), "norm_ffn_g": norm_ffn_g[l],
        "w_router": jnp.pad(w_router[l], ((0, 0), (0, LANES - N_EXPERTS))).astype(BF16),
        "b_router": jnp.pad(b_router[l], (0, LANES - N_EXPERTS), constant_values=NEG_BIG).reshape(1, LANES),
        "w_gu": w_gu[l].astype(BF16), "b_gu": b_gu[l].reshape(N_EXPERTS, 1, 2 * D_EXPERT),
        "w_down": w_down[l].astype(BF16), "b_down": b_down[l].reshape(N_EXPERTS, 1, D_MODEL),
        "norm_ple_g": norm_ple_g[l], "w_ple_gate": w_ple_gate[l].astype(BF16),
        "w_ple_proj": w_ple_proj[l].astype(BF16), "ple_post_g": ple_post_g[l], "final_g": final_g,
    }
    y_prompt = _trunk(x_prompt, p_prompt[l], wts)
    y_sample = _trunk(x_sample, p_sample[l], wts)
    return (y_prompt, y_sample)
```

```python
import jax
import jax.numpy as jnp
from jax import lax
from jax.experimental import pallas as pl
from jax.experimental.pallas import tpu as pltpu

D_MODEL = 1024
D_IN = 6 * D_MODEL
RNN_HEADS = 16
RNN_HEAD_DIM = D_MODEL // RNN_HEADS
RNN_CONV = 4
RG_C = 8.0
CONV_WIDTH = 31
N_EXPERTS = 32
TOP_K = 4
D_EXPERT = D_MODEL
SWIGLU_LIMIT = 7.0
SWIGLU_ALPHA = 1.702
D_PLE = 256
EPS = 1e-6

LANES = 128
SUBLANES = 8
NEG_BIG = -1e30

F32 = jnp.float32
BF16 = jnp.bfloat16

VMEM_LIMIT = 56 * 1024 * 1024


def _cparams(sem):
    return pltpu.CompilerParams(dimension_semantics=sem, vmem_limit_bytes=VMEM_LIMIT)


def _rms(x, g):
    return x * lax.rsqrt(jnp.mean(x * x, axis=-1, keepdims=True) + EPS) * g


def _sigmoid(x):
    return 0.5 * jnp.tanh(0.5 * x) + 0.5


def _inproj_kernel(x_ref, g_ref, w_ref, b_ref, z_ref, h_scr):
    @pl.when(pl.program_id(1) == 0)
    def _():
        h_scr[...] = _rms(x_ref[...], g_ref[...]).astype(BF16)

    z_ref[...] = jnp.dot(h_scr[...], w_ref[...], preferred_element_type=F32) + b_ref[...]


def _inproj(x, g, w, b, tm=1024, tn=2048):
    t = x.shape[0]
    return pl.pallas_call(
        _inproj_kernel,
        grid=(t // tm, D_IN // tn),
        in_specs=[
            pl.BlockSpec((tm, D_MODEL), lambda i, j: (i, 0)),
            pl.BlockSpec((1, D_MODEL), lambda i, j: (0, 0)),
            pl.BlockSpec((D_MODEL, tn), lambda i, j: (0, j)),
            pl.BlockSpec((1, tn), lambda i, j: (0, j)),
        ],
        out_specs=pl.BlockSpec((tm, tn), lambda i, j: (i, j)),
        out_shape=jax.ShapeDtypeStruct((t, D_IN), F32),
        scratch_shapes=[pltpu.VMEM((tm, D_MODEL), BF16)],
        compiler_params=_cparams(("arbitrary", "arbitrary")),
        name="inproj",
    )(x, g, w, b)


RNN_CB = 256
RNN_CHUNK = 128


def _rnn_kernel(xr_ref, yr_ref, cw_ref, cb_ref, wa_ref, ba_ref, wx_ref, bx_ref, c_ref, o_ref, xpad, hf):
    s, cb = xr_ref.shape
    chunk = RNN_CHUNK
    nchunks = s // chunk
    zeros8 = jnp.zeros((SUBLANES, LANES), F32)
    for c in range(cb // LANES):
        xpad[c, 0:SUBLANES, :] = zeros8
        xpad[c, s + SUBLANES:s + 2 * SUBLANES, :] = zeros8
        xpad[c, SUBLANES:s + SUBLANES, :] = xr_ref[:, c * LANES:(c + 1) * LANES]
    row8 = lax.broadcasted_iota(jnp.int32, (SUBLANES, cb), 0)

    def conv(t0):
        parts = []
        for c in range(cb // LANES):
            lanes = slice(c * LANES, (c + 1) * LANES)
            acc = cb_ref[:, lanes] + cw_ref[0:1, lanes] * xpad[c, pl.ds(t0 + 6, chunk), :]
            for k in range(1, RNN_CONV):
                acc = acc + cw_ref[k:k + 1, lanes] * xpad[c, pl.ds(t0 + (6 + k), chunk), :]
            parts.append(acc)
        return jnp.concatenate(parts, axis=1)

    def gates(xc, d):
        xb = xc.astype(BF16)
        r = _sigmoid(jnp.dot(xb, wa_ref[d], preferred_element_type=F32) + ba_ref[d:d + 1, :])
        ig = _sigmoid(jnp.dot(xb, wx_ref[d], preferred_element_type=F32) + bx_ref[d:d + 1, :])
        a = jnp.exp(-c_ref[d:d + 1, :] * r)
        u = jnp.sqrt(1.0 - a * a) * (ig * xc)
        return a, u

    def tile_scan(a, u, reverse):
        for dist in (1, 2, 4):
            if reverse:
                keep = row8 < SUBLANES - dist
                shift = SUBLANES - dist
            else:
                keep = row8 >= dist
                shift = dist
            us = jnp.where(keep, pltpu.roll(u, shift, 0), 0.0)
            a_s = jnp.where(keep, pltpu.roll(a, shift, 0), 1.0)
            u = u + a * us
            a = a * a_s
        return a, u

    def fwd_body(i, h):
        t0 = pl.multiple_of(i * chunk, chunk)
        a, u = gates(conv(t0), 0)
        tiles = []
        for j in range(chunk // SUBLANES):
            at, ut = tile_scan(a[j * 8:(j + 1) * 8], u[j * 8:(j + 1) * 8], False)
            ht = ut + at * h
            h = ht[7:8, :]
            tiles.append(ht)
        hf[pl.ds(t0, chunk), :] = jnp.concatenate(tiles, axis=0)
        return h

    lax.fori_loop(0, nchunks, fwd_body, jnp.zeros((1, cb), F32), unroll=4)

    def bwd_body(i, h):
        t0 = pl.multiple_of((nchunks - 1 - i) * chunk, chunk)
        a, u = gates(conv(t0), 1)
        tiles = [None] * (chunk // SUBLANES)
        for j in reversed(range(chunk // SUBLANES)):
            at, ut = tile_scan(a[j * 8:(j + 1) * 8], u[j * 8:(j + 1) * 8], True)
            ht = ut + at * h
            h = ht[0:1, :]
            tiles[j] = ht
        hb = jnp.concatenate(tiles, axis=0)
        y = yr_ref[pl.ds(t0, chunk), :]
        o_ref[pl.ds(t0, chunk), :] = ((hf[pl.ds(t0, chunk), :] + hb) * jax.nn.gelu(y)).astype(o_ref.dtype)
        return h

    lax.fori_loop(0, nchunks, bwd_body, jnp.zeros((1, cb), F32), unroll=4)


def _rnn(z3, cw, cb_, wa_bd, ba, wx_bd, bx, c):
    b, s, _ = z3.shape
    nc = D_MODEL // RNN_CB
    return pl.pallas_call(
        _rnn_kernel,
        grid=(b, nc),
        in_specs=[
            pl.BlockSpec((None, s, RNN_CB), lambda i, j: (i, 0, j)),
            pl.BlockSpec((None, s, RNN_CB), lambda i, j: (i, 0, nc + j)),
            pl.BlockSpec((RNN_CONV, RNN_CB), lambda i, j: (0, j)),
            pl.BlockSpec((1, RNN_CB), lambda i, j: (0, j)),
            pl.BlockSpec((2, None, RNN_CB, RNN_CB), lambda i, j: (0, j, 0, 0)),
            pl.BlockSpec((2, RNN_CB), lambda i, j: (0, j)),
            pl.BlockSpec((2, None, RNN_CB, RNN_CB), lambda i, j: (0, j, 0, 0)),
            pl.BlockSpec((2, RNN_CB), lambda i, j: (0, j)),
            pl.BlockSpec((2, RNN_CB), lambda i, j: (0, j)),
        ],
        out_specs=pl.BlockSpec((None, s, RNN_CB), lambda i, j: (i, 0, j)),
        out_shape=jax.ShapeDtypeStruct((b, s, D_MODEL), BF16),
        scratch_shapes=[pltpu.VMEM((RNN_CB // LANES, s + 2 * SUBLANES, LANES), F32),
                        pltpu.VMEM((s, RNN_CB), F32)],
        compiler_params=_cparams(("arbitrary", "arbitrary")),
        name="rnn",
    )(z3, z3, cw, cb_, wa_bd, ba, wx_bd, bx, c)


MIX_TS = 512
MIX_HALO = 16
MIX_RC = 64


def _mix_kernel(cv_ref, cg_ref, cvp_ref, cgp_ref, cvn_ref, cgn_ref, g_ref, gr_ref, gc_ref, x_ref,
                dw_ref, db_ref, lng_ref, lnb_ref, wcp_ref, bcp_ref, wrp_ref, wo_ref, nfg_ref, wr_ref, br_ref,
                x1_ref, h2_ref, ridx_ref, rgate_ref, uext, cscr):
    ts = cv_ref.shape[0]
    i = pl.program_id(1)
    last = pl.num_programs(1) - 1
    up = jnp.where(i > 0, cvp_ref[...] * _sigmoid(cgp_ref[...]), 0.0)
    un = jnp.where(i < last, cvn_ref[...] * _sigmoid(cgn_ref[...]), 0.0)
    um = cv_ref[...] * _sigmoid(cg_ref[...])
    for c in range(D_MODEL // LANES):
        lanes = slice(c * LANES, (c + 1) * LANES)
        uext[c, 0:MIX_HALO, :] = up[:, lanes]
        uext[c, MIX_HALO:MIX_HALO + ts, :] = um[:, lanes]
        uext[c, MIX_HALO + ts:2 * MIX_HALO + ts, :] = un[:, lanes]

    off = MIX_HALO - CONV_WIDTH // 2

    rnn = jnp.dot(g_ref[...], wrp_ref[...], preferred_element_type=F32)

    for rc in range(ts // MIX_RC):
        r0 = rc * MIX_RC
        for c in range(D_MODEL // LANES):
            lanes = slice(c * LANES, (c + 1) * LANES)
            acc = db_ref[:, lanes] + dw_ref[0:1, lanes] * uext[c, pl.ds(r0 + off, MIX_RC), :]
            for k in range(1, CONV_WIDTH):
                acc = acc + dw_ref[k:k + 1, lanes] * uext[c, pl.ds(r0 + (off + k), MIX_RC), :]
            cscr[pl.ds(r0, MIX_RC), lanes] = acc

    c = cscr[...]
    mu = jnp.mean(c, axis=-1, keepdims=True)
    cc = c - mu
    var = jnp.mean(cc * cc, axis=-1, keepdims=True)
    y = cc * lax.rsqrt(var + EPS) * lng_ref[...] + lnb_ref[...]
    sw = y * _sigmoid(y)
    conv = jnp.dot(sw.astype(BF16), wcp_ref[...], preferred_element_type=F32) + bcp_ref[...]
    merged = _sigmoid(gr_ref[...]) * rnn + _sigmoid(gc_ref[...]) * conv
    x1 = x_ref[...] + jnp.dot(merged.astype(BF16), wo_ref[...], preferred_element_type=F32)
    x1_ref[...] = x1
    h2 = _rms(x1, nfg_ref[...])
    h2_ref[...] = h2

    logits = jnp.dot(h2.astype(BF16), wr_ref[...], preferred_element_type=F32) + br_ref[...]
    lane = lax.broadcasted_iota(jnp.int32, logits.shape, 1)
    vals, idxs = [], []
    for _ in range(TOP_K):
        m = jnp.max(logits, axis=-1, keepdims=True)
        idx = jnp.min(jnp.where(logits == m, lane, LANES), axis=-1, keepdims=True)
        vals.append(m)
        idxs.append(idx)
        logits = jnp.where(lane == idx, 2.0 * NEG_BIG, logits)
    es = [jnp.exp(v - vals[0]) for v in vals]
    inv = 1.0 / (es[0] + es[1] + es[2] + es[3])
    ridx = jnp.zeros(logits.shape, jnp.int32)
    rgate = jnp.zeros(logits.shape, F32)
    for k in range(TOP_K):
        ridx = jnp.where(lane == k, idxs[k], ridx)
        rgate = jnp.where(lane == k, es[k] * inv, rgate)
    ridx_ref[...] = ridx
    rgate_ref[...] = rgate


def _mix(z3, g3, x3, dw, db, lng, lnb, wcp, bcp, wrp, wo, nfg, wr, br):
    b, s, _ = z3.shape
    ts = MIX_TS
    nt = s // ts
    hb = ts // MIX_HALO
    nhb = s // MIX_HALO
    tile = lambda col: pl.BlockSpec((None, ts, D_MODEL), lambda i, j, col=col: (i, j, col))
    prev = lambda col: pl.BlockSpec((None, MIX_HALO, D_MODEL),
                                    lambda i, j, col=col: (i, jnp.maximum(j * hb - 1, 0), col))
    nxt = lambda col: pl.BlockSpec((None, MIX_HALO, D_MODEL),
                                   lambda i, j, col=col: (i, jnp.minimum((j + 1) * hb, nhb - 1), col))
    full = lambda shape: pl.BlockSpec(shape, lambda i, j: (0,) * len(shape))
    out_tile = lambda w: pl.BlockSpec((None, ts, w), lambda i, j: (i, j, 0))
    return pl.pallas_call(
        _mix_kernel,
        grid=(b, nt),
        in_specs=[
            tile(2), tile(3), prev(2), prev(3), nxt(2), nxt(3),
            tile(0),
            tile(4), tile(5),
            tile(0),
            full((CONV_WIDTH, D_MODEL)), full((1, D_MODEL)), full((1, D_MODEL)), full((1, D_MODEL)),
            full((D_MODEL, D_MODEL)), full((1, D_MODEL)), full((D_MODEL, D_MODEL)), full((D_MODEL, D_MODEL)),
            full((1, D_MODEL)), full((D_MODEL, LANES)), full((1, LANES)),
        ],
        out_specs=[out_tile(D_MODEL), out_tile(D_MODEL), out_tile(LANES), out_tile(LANES)],
        out_shape=[
            jax.ShapeDtypeStruct((b, s, D_MODEL), F32),
            jax.ShapeDtypeStruct((b, s, D_MODEL), F32),
            jax.ShapeDtypeStruct((b, s, LANES), jnp.int32),
            jax.ShapeDtypeStruct((b, s, LANES), F32),
        ],
        scratch_shapes=[pltpu.VMEM((D_MODEL // LANES, ts + 2 * MIX_HALO, LANES), F32),
                        pltpu.VMEM((ts, D_MODEL), F32)],
        compiler_params=_cparams(("arbitrary", "arbitrary")),
        name="mix",
    )(z3, z3, z3, z3, z3, z3, g3, z3, z3, x3, dw, db, lng, lnb, wcp, bcp, wrp, wo, nfg, wr, br)


MOE_BM = 256
MOE_TT = 256
MOE_PER = TOP_K * MOE_TT
CHUNKS = D_MODEL // LANES


def _to_flat(flat_ref, lead, x):
    rows = x.shape[0]
    for c in range(CHUNKS):
        flat_ref[lead + (pl.ds(c, rows, stride=CHUNKS), slice(None))] = x[:, c * LANES:(c + 1) * LANES]


def _from_flat(flat_ref, lead, rows):
    return jnp.concatenate(
        [flat_ref[lead + (pl.ds(c, rows, stride=CHUNKS), slice(None))] for c in range(CHUNKS)], axis=1)


def _run_copies(cnt_ref, off_ref, glob_ref, tile, make):
    for e in range(N_EXPERTS):
        k = tile * N_EXPERTS + e
        n = cnt_ref[k]

        @pl.when(n > 0)
        def _(k=k, n=n):
            make(pl.multiple_of(off_ref[k] * CHUNKS, CHUNKS), pl.multiple_of(glob_ref[k] * CHUNKS, CHUNKS),
                 pl.multiple_of(n * CHUNKS, CHUNKS)).start()


def _disp_kernel(cnt_ref, off_ref, glob_ref, zst_ref, zn_ref, ord_ref, h_ref, xs_hbm, hflat, cbuf, zbuf, sem, zsem):
    i = pl.program_id(0)
    last = pl.num_programs(0) - 1
    slot = i % 2

    def whole(s):
        return pltpu.make_async_copy(cbuf.at[s], xs_hbm.at[pl.ds(0, MOE_PER * CHUNKS)], sem.at[s])

    @pl.when(i >= 2)
    def _():
        whole(slot).wait()

    _to_flat(hflat, (), h_ref[...])

    def compact(it, carry):
        for u in range(SUBLANES):
            j = it * SUBLANES + u
            src = pl.multiple_of(ord_ref[0, 0, j] * CHUNKS, CHUNKS)
            cbuf[slot, pl.ds(pl.multiple_of(j * CHUNKS, CHUNKS), CHUNKS), :] = hflat[pl.ds(src, CHUNKS), :]
        return carry

    lax.fori_loop(0, MOE_PER // SUBLANES, compact, 0)

    _run_copies(cnt_ref, off_ref, glob_ref, i, lambda loc, glob, n: pltpu.make_async_copy(
        cbuf.at[slot, pl.ds(loc, n)], xs_hbm.at[pl.ds(glob, n)], sem.at[slot]))

    @pl.when(i == last)
    def _():
        zbuf[...] = jnp.zeros(zbuf.shape, zbuf.dtype)
        for e in range(N_EXPERTS):
            n = zn_ref[e]

            @pl.when(n > 0)
            def _(e=e, n=n):
                rows = pl.multiple_of(n * CHUNKS, CHUNKS)
                cp = pltpu.make_async_copy(zbuf.at[pl.ds(0, rows)],
                                           xs_hbm.at[pl.ds(pl.multiple_of(zst_ref[e] * CHUNKS, CHUNKS), rows)], zsem)
                cp.start()
                cp.wait()

        def zero_block(k, carry):
            row0 = pl.multiple_of((zst_ref[N_EXPERTS] + k * MOE_BM) * CHUNKS, CHUNKS)
            cp = pltpu.make_async_copy(zbuf, xs_hbm.at[pl.ds(row0, MOE_BM * CHUNKS)], zsem)
            cp.start()
            cp.wait()
            return carry

        lax.fori_loop(0, zn_ref[N_EXPERTS], zero_block, 0)

        @pl.when(i >= 1)
        def _():
            whole(1 - slot).wait()

        whole(slot).wait()


def _dispatch(h2, tables, ord_local, n_rows):
    t = h2.shape[0]
    nt = t // MOE_TT
    cnt, off, glob, zst, zn = tables
    grid_spec = pltpu.PrefetchScalarGridSpec(
        num_scalar_prefetch=5,
        grid=(nt,),
        in_specs=[
            pl.BlockSpec((1, 1, MOE_PER), lambda i, *_: (i, 0, 0), memory_space=pltpu.SMEM),
            pl.BlockSpec((MOE_TT, D_MODEL), lambda i, *_: (i, 0)),
        ],
        out_specs=pl.BlockSpec(memory_space=pl.ANY),
        scratch_shapes=[
            pltpu.VMEM((MOE_TT * CHUNKS, LANES), F32),
            pltpu.VMEM((2, MOE_PER * CHUNKS, LANES), F32),
            pltpu.VMEM((MOE_BM * CHUNKS, LANES), F32),
            pltpu.SemaphoreType.DMA((2,)),
            pltpu.SemaphoreType.DMA(()),
        ],
    )
    return pl.pallas_call(
        _disp_kernel,
        grid_spec=grid_spec,
        out_shape=jax.ShapeDtypeStruct((n_rows * CHUNKS, LANES), F32),
        compiler_params=_cparams(("arbitrary",)),
        name="dispatch",
    )(cnt, off, glob, zst, zn, ord_local, h2)


def _experts_kernel(be_ref, nused_ref, xs_ref, wgu0_ref, bgu0_ref, wd0_ref, bd0_ref, wgu1_ref, bgu1_ref, wd1_ref,
                    bd1_ref, ys_ref):
    del be_ref
    bm = MOE_BM
    first = 2 * pl.program_id(0)

    @pl.when(first < nused_ref[0])
    def _():
        for j, (wgu_ref, bgu_ref, wd_ref, bd_ref) in enumerate(((wgu0_ref, bgu0_ref, wd0_ref, bd0_ref),
                                                               (wgu1_ref, bgu1_ref, wd1_ref, bd1_ref))):
            rows = xs_ref.at[pl.ds(j * bm * CHUNKS, bm * CHUNKS)]
            xb = _from_flat(rows, (), bm).astype(BF16)
            gu = jnp.dot(xb, wgu_ref[...], preferred_element_type=F32) + bgu_ref[...]
            gt = jnp.minimum(gu[:, :D_EXPERT], SWIGLU_LIMIT)
            upv = jnp.clip(gu[:, D_EXPERT:], -SWIGLU_LIMIT, SWIGLU_LIMIT)
            act = (upv + 1.0) * (gt * _sigmoid(gt * SWIGLU_ALPHA))
            y = jnp.dot(act.astype(BF16), wd_ref[...], preferred_element_type=F32) + bd_ref[...]
            _to_flat(ys_ref.at[pl.ds(j * bm * CHUNKS, bm * CHUNKS)], (), y)

    @pl.when(first >= nused_ref[0])
    def _():
        ys_ref[...] = jnp.zeros(ys_ref.shape, ys_ref.dtype)


def _experts(xs, block_e, n_used, wgu, bgu, wd, bd):
    nb = block_e.shape[0]
    bm = MOE_BM
    wspec = lambda shape, j: pl.BlockSpec((None,) + shape, lambda i, be, nu, j=j: (be[2 * i + j], 0, 0))
    weights = lambda j: [wspec((D_MODEL, 2 * D_EXPERT), j), wspec((1, 2 * D_EXPERT), j),
                         wspec((D_EXPERT, D_MODEL), j), wspec((1, D_MODEL), j)]
    grid_spec = pltpu.PrefetchScalarGridSpec(
        num_scalar_prefetch=2,
        grid=(nb // 2,),
        in_specs=[pl.BlockSpec((2 * bm * CHUNKS, LANES), lambda i, be, nu: (i, 0))] + weights(0) + weights(1),
        out_specs=pl.BlockSpec((2 * bm * CHUNKS, LANES), lambda i, be, nu: (i, 0)),
    )
    return pl.pallas_call(
        _experts_kernel,
        grid_spec=grid_spec,
        out_shape=jax.ShapeDtypeStruct(xs.shape, F32),
        compiler_params=_cparams(("arbitrary",)),
        name="experts",
    )(block_e, n_used, xs, wgu, bgu, wd, bd, wgu, bgu, wd, bd)


def _combine_kernel(cnt_ref, off_ref, glob_ref, x1_ref, tok_ref, gate_ref, p_ref, npg_ref, wpg_ref, wpp_ref, ppg_ref,
                    fg_ref, ys_hbm, o_ref, gbuf, sem):
    i = pl.program_id(0)
    nt = pl.num_programs(0)
    slot = i % 2

    def fetch(tile, s):
        _run_copies(cnt_ref, off_ref, glob_ref, tile, lambda loc, glob, n: pltpu.make_async_copy(
            ys_hbm.at[pl.ds(glob, n)], gbuf.at[s, pl.ds(loc, n)], sem.at[s]))

    @pl.when(i == 0)
    def _():
        fetch(0, 0)

    @pl.when(i + 1 < nt)
    def _():
        fetch(i + 1, 1 - slot)

    pltpu.make_async_copy(ys_hbm.at[pl.ds(0, MOE_PER * CHUNKS)], gbuf.at[slot], sem.at[slot]).wait()

    y = _from_flat(gbuf, (slot,), MOE_PER).astype(BF16)
    tok_of_row = tok_ref[0]
    hit = lax.broadcasted_iota(jnp.int32, (MOE_TT, MOE_PER), 0) == tok_of_row
    sel = jnp.where(hit, gate_ref[0], 0.0).astype(BF16)
    x2 = x1_ref[...] + jnp.dot(sel, y, preferred_element_type=F32)
    h = _rms(x2, npg_ref[...])
    gate = _sigmoid(jnp.dot(h.astype(BF16), wpg_ref[...], preferred_element_type=F32))
    emb = _rms(jnp.dot(p_ref[...].astype(BF16), wpp_ref[...], preferred_element_type=F32), ppg_ref[...])
    x3 = x2 + gate * emb
    o_ref[...] = _rms(x3, fg_ref[...])


def _combine(x1, ys, tables, tok_of_row, gate_of_row, p, npg, wpg, wpp, ppg, fg):
    t = x1.shape[0]
    tm = MOE_TT
    nt = t // tm
    cnt, off, glob = tables
    full = lambda shape: pl.BlockSpec(shape, lambda i, *_: (0,) * len(shape))
    grid_spec = pltpu.PrefetchScalarGridSpec(
        num_scalar_prefetch=3,
        grid=(nt,),
        in_specs=[
            pl.BlockSpec((tm, D_MODEL), lambda i, *_: (i, 0)),
            pl.BlockSpec((1, 1, MOE_PER), lambda i, *_: (i, 0, 0)),
            pl.BlockSpec((1, 1, MOE_PER), lambda i, *_: (i, 0, 0)),
            pl.BlockSpec((tm, D_PLE), lambda i, *_: (i, 0)),
            full((1, D_MODEL)), full((D_MODEL, D_MODEL)), full((D_PLE, D_MODEL)), full((1, D_MODEL)),
            full((1, D_MODEL)),
            pl.BlockSpec(memory_space=pl.ANY),
        ],
        out_specs=pl.BlockSpec((tm, D_MODEL), lambda i, *_: (i, 0)),
        scratch_shapes=[pltpu.VMEM((2, MOE_PER * CHUNKS, LANES), F32), pltpu.SemaphoreType.DMA((2,))],
    )
    return pl.pallas_call(
        _combine_kernel,
        grid_spec=grid_spec,
        out_shape=jax.ShapeDtypeStruct((t, D_MODEL), F32),
        compiler_params=_cparams(("arbitrary",)),
        name="combine",
    )(cnt, off, glob, x1, tok_of_row, gate_of_row, p, npg, wpg, wpp, ppg, fg, ys)


def _route_tiles(ridx, rgate, bm):
    t = ridx.shape[0]
    tk = t * TOP_K
    nt = t // MOE_TT
    flat_e = ridx[:, :TOP_K].reshape(-1)
    flat_g = rgate[:, :TOP_K].reshape(-1)
    pos = jnp.arange(tk, dtype=jnp.int32)
    tile_of = pos // MOE_PER
    order = jnp.argsort(tile_of * N_EXPERTS + flat_e, stable=True).astype(jnp.int32)
    tok_local = (lax.shift_right_logical(order, 2) - tile_of * MOE_TT).astype(jnp.int32)
    gate_of_row = flat_g[order]
    experts = jnp.arange(N_EXPERTS, dtype=jnp.int32)
    cnt = jnp.sum((flat_e.reshape(nt, MOE_PER, 1) == experts[None, None, :]).astype(jnp.int32), axis=1)
    off = jnp.cumsum(cnt, axis=1) - cnt
    counts = jnp.sum(cnt, axis=0)
    nblk = (counts + bm - 1) // bm
    blk_end = jnp.cumsum(nblk)
    pad_start = (blk_end - nblk) * bm
    glob = pad_start[None, :] + jnp.cumsum(cnt, axis=0) - cnt
    n_blocks = tk // bm + N_EXPERTS
    b = jnp.arange(n_blocks, dtype=jnp.int32)
    block_e = jnp.minimum(jnp.sum((b[:, None] >= blk_end[None, :]).astype(jnp.int32), axis=1), N_EXPERTS - 1)
    flat = lambda v: v.reshape(-1).astype(jnp.int32)
    tables = (flat(cnt), flat(off), flat(glob))
    zero_runs = (flat(jnp.concatenate([pad_start + counts, blk_end[-1:] * bm])),
                 flat(jnp.concatenate([nblk * bm - counts, n_blocks - blk_end[-1:]])))
    shape = (nt, 1, MOE_PER)
    return (tables, zero_runs, tok_local.reshape(shape), gate_of_row.reshape(shape), block_e.astype(jnp.int32),
            blk_end[-1:].astype(jnp.int32), n_blocks * bm)


def _block_diag(w):
    per = RNN_CB // RNN_HEAD_DIM
    groups = D_MODEL // RNN_CB
    w = w.reshape(2, groups, per, RNN_HEAD_DIM, RNN_HEAD_DIM)
    eye = jnp.eye(per, dtype=w.dtype)
    bd = jnp.einsum("dgpij,pq->dgpiqj", w, eye)
    return bd.reshape(2, groups, RNN_CB, RNN_CB)


def _trunk(x, p, wts):
    b, s, d = x.shape
    t = b * s
    row = lambda v: v.reshape(1, -1)
    z = _inproj(x.reshape(t, d), row(wts["norm_mix_g"]), wts["w_in"], row(wts["b_in"]))
    z3 = z.reshape(b, s, D_IN)
    g3 = _rnn(z3, wts["conv_rnn_w"], row(wts["conv_rnn_b"]), wts["w_a_bd"], wts["b_a"], wts["w_x_bd"], wts["b_x"],
              wts["rg_c"])
    x1, h2, ridx, rgate = _mix(z3, g3, x, wts["conv_dw_w"], row(wts["conv_dw_b"]), row(wts["ln_g"]),
                               row(wts["ln_b"]), wts["w_conv_proj"], row(wts["b_conv_proj"]), wts["w_rnn_proj"],
                               wts["w_out"], row(wts["norm_ffn_g"]), wts["w_router"], wts["b_router"])
    tables, zero_runs, tok_of_row, gate_of_row, block_e, n_used, n_rows = _route_tiles(
        ridx.reshape(t, LANES), rgate.reshape(t, LANES), MOE_BM)
    xs = _dispatch(h2.reshape(t, d), tables + zero_runs, tok_of_row, n_rows)
    ys = _experts(xs, block_e, n_used, wts["w_gu"], wts["b_gu"], wts["w_down"], wts["b_down"])
    y = _combine(x1.reshape(t, d), ys, tables, tok_of_row, gate_of_row, p.reshape(t, D_PLE),
                 row(wts["norm_ple_g"]), wts["w_ple_gate"], wts["w_ple_proj"], row(wts["ple_post_g"]),
                 row(wts["final_g"]))
    return y.reshape(b, s, d)


def kernel(x_prompt, x_sample, p_prompt, p_sample, norm_mix_g, w_in, b_in, conv_rnn_w, conv_rnn_b, w_a, b_a, w_x,
           b_x, lam, w_rnn_proj, conv_dw_w, conv_dw_b, ln_g, ln_b, w_conv_proj, b_conv_proj, w_out, norm_ffn_g,
           w_router, b_router, w_gu, b_gu, w_down, b_down, norm_ple_g, w_ple_gate, w_ple_proj, ple_post_g, final_g):
    l = 0
    wts = {
        "norm_mix_g": norm_mix_g[l], "w_in": w_in[l].astype(BF16), "b_in": b_in[l],
        "conv_rnn_w": conv_rnn_w[l], "conv_rnn_b": conv_rnn_b[l],
        "w_a_bd": _block_diag(w_a[l]).astype(BF16), "b_a": b_a[l],
        "w_x_bd": _block_diag(w_x[l]).astype(BF16), "b_x": b_x[l],
        "rg_c": RG_C * jax.nn.softplus(-lam[l]),
        "w_rnn_proj": w_rnn_proj[l].astype(BF16),
        "conv_dw_w": conv_dw_w[l], "conv_dw_b": conv_dw_b[l], "ln_g": ln_g[l], "ln_b": ln_b[l],
        "w_conv_proj": w_conv_proj[l].astype(BF16), "b_conv_proj": b_conv_proj[l],
        "w_out": w_out[l].astype(BF16), "norm_ffn_g": norm_ffn_g[l],
        "w_router": jnp.pad(w_router[l], ((0, 0), (0, LANES - N_EXPERTS))).astype(BF16),
        "b_router": jnp.pad(b_router[l], (0, LANES - N_EXPERTS), constant_values=NEG_BIG).reshape(1, LANES),
        "w_gu": w_gu[l].astype(BF16), "b_gu": b_gu[l].reshape(N_EXPERTS, 1, 2 * D_EXPERT),
        "w_down": w_down[l].astype(BF16), "b_down": b_down[l].reshape(N_EXPERTS, 1, D_MODEL),
        "norm_ple_g": norm_ple_g[l], "w_ple_gate": w_ple_gate[l].astype(BF16),
        "w_ple_proj": w_ple_proj[l].astype(BF16), "ple_post_g": ple_post_g[l], "final_g": final_g,
    }
    y_prompt = _trunk(x_prompt, p_prompt[l], wts)
    y_sample = _trunk(x_sample, p_sample[l], wts)
    return (y_prompt, y_sample)
```

```python
import jax
import jax.numpy as jnp
from jax import lax
from jax.experimental import pallas as pl
from jax.experimental.pallas import tpu as pltpu

D_MODEL = 1024
D_IN = 6 * D_MODEL
RNN_HEADS = 16
RNN_HEAD_DIM = D_MODEL // RNN_HEADS
RNN_CONV = 4
RG_C = 8.0
CONV_WIDTH = 31
N_EXPERTS = 32
TOP_K = 4
D_EXPERT = D_MODEL
SWIGLU_LIMIT = 7.0
SWIGLU_ALPHA = 1.702
D_PLE = 256
EPS = 1e-6

LANES = 128
SUBLANES = 8
NEG_INF = float("-inf")

F32 = jnp.float32
BF16 = jnp.bfloat16

VMEM_LIMIT = 56 * 1024 * 1024


def _cparams(sem):
    return pltpu.CompilerParams(dimension_semantics=sem, vmem_limit_bytes=VMEM_LIMIT)


def _rms(x, g):
    return x * lax.rsqrt(jnp.mean(x * x, axis=-1, keepdims=True) + EPS) * g


def _sigmoid(x):
    return 0.5 * jnp.tanh(0.5 * x) + 0.5


def _inproj_kernel(x_ref, g_ref, w_ref, b_ref, z_ref, h_scr):
    @pl.when(pl.program_id(1) == 0)
    def _():
        h_scr[...] = _rms(x_ref[...], g_ref[...]).astype(BF16)

    z_ref[...] = jnp.dot(h_scr[...], w_ref[...], preferred_element_type=F32) + b_ref[...]


def _inproj(x, g, w, b, tm=1024, tn=2048):
    t = x.shape[0]
    return pl.pallas_call(
        _inproj_kernel,
        grid=(t // tm, D_IN // tn),
        in_specs=[
            pl.BlockSpec((tm, D_MODEL), lambda i, j: (i, 0)),
            pl.BlockSpec((1, D_MODEL), lambda i, j: (0, 0)),
            pl.BlockSpec((D_MODEL, tn), lambda i, j: (0, j)),
            pl.BlockSpec((1, tn), lambda i, j: (0, j)),
        ],
        out_specs=pl.BlockSpec((tm, tn), lambda i, j: (i, j)),
        out_shape=jax.ShapeDtypeStruct((t, D_IN), F32),
        scratch_shapes=[pltpu.VMEM((tm, D_MODEL), BF16)],
        compiler_params=_cparams(("arbitrary", "arbitrary")),
        name="inproj",
    )(x, g, w, b)


RNN_CB = 256
RNN_CHUNK = 128


def _rnn_kernel(xr_ref, yr_ref, cw_ref, cb_ref, wa_ref, ba_ref, wx_ref, bx_ref, c_ref, o_ref, xpad, hf):
    s, cb = xr_ref.shape
    chunk = RNN_CHUNK
    nchunks = s // chunk
    zeros8 = jnp.zeros((SUBLANES, LANES), F32)
    for c in range(cb // LANES):
        xpad[c, 0:SUBLANES, :] = zeros8
        xpad[c, s + SUBLANES:s + 2 * SUBLANES, :] = zeros8
        xpad[c, SUBLANES:s + SUBLANES, :] = xr_ref[:, c * LANES:(c + 1) * LANES]
    row8 = lax.broadcasted_iota(jnp.int32, (SUBLANES, cb), 0)

    def conv(t0):
        parts = []
        for c in range(cb // LANES):
            lanes = slice(c * LANES, (c + 1) * LANES)
            acc = cb_ref[:, lanes] + cw_ref[0:1, lanes] * xpad[c, pl.ds(t0 + 6, chunk), :]
            for k in range(1, RNN_CONV):
                acc = acc + cw_ref[k:k + 1, lanes] * xpad[c, pl.ds(t0 + (6 + k), chunk), :]
            parts.append(acc)
        return jnp.concatenate(parts, axis=1)

    def gates(xc, d):
        xb = xc.astype(BF16)
        r = _sigmoid(jnp.dot(xb, wa_ref[d], preferred_element_type=F32) + ba_ref[d:d + 1, :])
        ig = _sigmoid(jnp.dot(xb, wx_ref[d], preferred_element_type=F32) + bx_ref[d:d + 1, :])
        a = jnp.exp(-c_ref[d:d + 1, :] * r)
        u = jnp.sqrt(1.0 - a * a) * (ig * xc)
        return a, u

    def tile_scan(a, u, reverse):
        for dist in (1, 2, 4):
            if reverse:
                keep = row8 < SUBLANES - dist
                shift = SUBLANES - dist
            else:
                keep = row8 >= dist
                shift = dist
            us = jnp.where(keep, pltpu.roll(u, shift, 0), 0.0)
            a_s = jnp.where(keep, pltpu.roll(a, shift, 0), 1.0)
            u = u + a * us
            a = a * a_s
        return a, u

    def fwd_body(i, h):
        t0 = pl.multiple_of(i * chunk, chunk)
        a, u = gates(conv(t0), 0)
        tiles = []
        for j in range(chunk // SUBLANES):
            at, ut = tile_scan(a[j * 8:(j + 1) * 8], u[j * 8:(j + 1) * 8], False)
            ht = ut + at * h
            h = ht[7:8, :]
            tiles.append(ht)
        hf[pl.ds(t0, chunk), :] = jnp.concatenate(tiles, axis=0)
        return h

    lax.fori_loop(0, nchunks, fwd_body, jnp.zeros((1, cb), F32), unroll=4)

    def bwd_body(i, h):
        t0 = pl.multiple_of((nchunks - 1 - i) * chunk, chunk)
        a, u = gates(conv(t0), 1)
        tiles = [None] * (chunk // SUBLANES)
        for j in reversed(range(chunk // SUBLANES)):
            at, ut = tile_scan(a[j * 8:(j + 1) * 8], u[j * 8:(j + 1) * 8], True)
            ht = ut + at * h
            h = ht[0:1, :]
            tiles[j] = ht
        hb = jnp.concatenate(tiles, axis=0)
        y = yr_ref[pl.ds(t0, chunk), :]
        o_ref[pl.ds(t0, chunk), :] = ((hf[pl.ds(t0, chunk), :] + hb) * jax.nn.gelu(y)).astype(o_ref.dtype)
        return h

    lax.fori_loop(0, nchunks, bwd_body, jnp.zeros((1, cb), F32), unroll=4)


def _rnn(z3, cw, cb_, wa_bd, ba, wx_bd, bx, c):
    b, s, _ = z3.shape
    nc = D_MODEL // RNN_CB
    return pl.pallas_call(
        _rnn_kernel,
        grid=(b, nc),
        in_specs=[
            pl.BlockSpec((None, s, RNN_CB), lambda i, j: (i, 0, j)),
            pl.BlockSpec((None, s, RNN_CB), lambda i, j: (i, 0, nc + j)),
            pl.BlockSpec((RNN_CONV, RNN_CB), lambda i, j: (0, j)),
            pl.BlockSpec((1, RNN_CB), lambda i, j: (0, j)),
            pl.BlockSpec((2, None, RNN_CB, RNN_CB), lambda i, j: (0, j, 0, 0)),
            pl.BlockSpec((2, RNN_CB), lambda i, j: (0, j)),
            pl.BlockSpec((2, None, RNN_CB, RNN_CB), lambda i, j: (0, j, 0, 0)),
            pl.BlockSpec((2, RNN_CB), lambda i, j: (0, j)),
            pl.BlockSpec((2, RNN_CB), lambda i, j: (0, j)),
        ],
        out_specs=pl.BlockSpec((None, s, RNN_CB), lambda i, j: (i, 0, j)),
        out_shape=jax.ShapeDtypeStruct((b, s, D_MODEL), BF16),
        scratch_shapes=[pltpu.VMEM((RNN_CB // LANES, s + 2 * SUBLANES, LANES), F32),
                        pltpu.VMEM((s, RNN_CB), F32)],
        compiler_params=_cparams(("arbitrary", "arbitrary")),
        name="rnn",
    )(z3, z3, cw, cb_, wa_bd, ba, wx_bd, bx, c)


MIX_TS = 512
MIX_HALO = 16
MIX_RC = 64


def _mix_kernel(cv_ref, cg_ref, cvp_ref, cgp_ref, cvn_ref, cgn_ref, g_ref, gr_ref, gc_ref, x_ref,
                dw_ref, db_ref, lng_ref, lnb_ref, wcp_ref, bcp_ref, wrp_ref, wo_ref, nfg_ref, wr_ref, br_ref,
                x1_ref, h2_ref, ridx_ref, rgate_ref, uext, cscr):
    ts = cv_ref.shape[0]
    i = pl.program_id(1)
    last = pl.num_programs(1) - 1
    up = jnp.where(i > 0, cvp_ref[...] * _sigmoid(cgp_ref[...]), 0.0)
    un = jnp.where(i < last, cvn_ref[...] * _sigmoid(cgn_ref[...]), 0.0)
    um = cv_ref[...] * _sigmoid(cg_ref[...])
    for c in range(D_MODEL // LANES):
        lanes = slice(c * LANES, (c + 1) * LANES)
        uext[c, 0:MIX_HALO, :] = up[:, lanes]
        uext[c, MIX_HALO:MIX_HALO + ts, :] = um[:, lanes]
        uext[c, MIX_HALO + ts:2 * MIX_HALO + ts, :] = un[:, lanes]

    off = MIX_HALO - CONV_WIDTH // 2

    rnn = jnp.dot(g_ref[...], wrp_ref[...], preferred_element_type=F32)

    for rc in range(ts // MIX_RC):
        r0 = rc * MIX_RC
        for c in range(D_MODEL // LANES):
            lanes = slice(c * LANES, (c + 1) * LANES)
            acc = db_ref[:, lanes] + dw_ref[0:1, lanes] * uext[c, pl.ds(r0 + off, MIX_RC), :]
            for k in range(1, CONV_WIDTH):
                acc = acc + dw_ref[k:k + 1, lanes] * uext[c, pl.ds(r0 + (off + k), MIX_RC), :]
            cscr[pl.ds(r0, MIX_RC), lanes] = acc

    c = cscr[...]
    mu = jnp.mean(c, axis=-1, keepdims=True)
    cc = c - mu
    var = jnp.mean(cc * cc, axis=-1, keepdims=True)
    y = cc * lax.rsqrt(var + EPS) * lng_ref[...] + lnb_ref[...]
    sw = y * _sigmoid(y)
    conv = jnp.dot(sw.astype(BF16), wcp_ref[...], preferred_element_type=F32) + bcp_ref[...]
    merged = _sigmoid(gr_ref[...]) * rnn + _sigmoid(gc_ref[...]) * conv
    x1 = x_ref[...] + jnp.dot(merged.astype(BF16), wo_ref[...], preferred_element_type=F32)
    x1_ref[...] = x1
    h2 = _rms(x1, nfg_ref[...])
    h2_ref[...] = h2

    logits = jnp.dot(h2.astype(BF16), wr_ref[...], preferred_element_type=F32) + br_ref[...]
    lane = lax.broadcasted_iota(jnp.int32, logits.shape, 1)
    vals, idxs = [], []
    for _ in range(TOP_K):
        m = jnp.max(logits, axis=-1, keepdims=True)
        idx = jnp.min(jnp.where(logits == m, lane, LANES), axis=-1, keepdims=True)
        vals.append(m)
        idxs.append(idx)
        logits = jnp.where(lane == idx, NEG_INF, logits)
    es = [jnp.exp(v - vals[0]) for v in vals]
    inv = 1.0 / (es[0] + es[1] + es[2] + es[3])
    ridx = jnp.zeros(logits.shape, jnp.int32)
    rgate = jnp.zeros(logits.shape, F32)
    for k in range(TOP_K):
        ridx = jnp.where(lane == k, idxs[k], ridx)
        rgate = jnp.where(lane == k, es[k] * inv, rgate)
    ridx_ref[...] = ridx
    rgate_ref[...] = rgate


def _mix(z3, g3, x3, dw, db, lng, lnb, wcp, bcp, wrp, wo, nfg, wr, br):
    b, s, _ = z3.shape
    ts = MIX_TS
    nt = s // ts
    hb = ts // MIX_HALO
    nhb = s // MIX_HALO
    tile = lambda col: pl.BlockSpec((None, ts, D_MODEL), lambda i, j, col=col: (i, j, col))
    prev = lambda col: pl.BlockSpec((None, MIX_HALO, D_MODEL),
                                    lambda i, j, col=col: (i, jnp.maximum(j * hb - 1, 0), col))
    nxt = lambda col: pl.BlockSpec((None, MIX_HALO, D_MODEL),
                                   lambda i, j, col=col: (i, jnp.minimum((j + 1) * hb, nhb - 1), col))
    full = lambda shape: pl.BlockSpec(shape, lambda i, j: (0,) * len(shape))
    out_tile = lambda w: pl.BlockSpec((None, ts, w), lambda i, j: (i, j, 0))
    return pl.pallas_call(
        _mix_kernel,
        grid=(b, nt),
        in_specs=[
            tile(2), tile(3), prev(2), prev(3), nxt(2), nxt(3),
            tile(0),
            tile(4), tile(5),
            tile(0),
            full((CONV_WIDTH, D_MODEL)), full((1, D_MODEL)), full((1, D_MODEL)), full((1, D_MODEL)),
            full((D_MODEL, D_MODEL)), full((1, D_MODEL)), full((D_MODEL, D_MODEL)), full((D_MODEL, D_MODEL)),
            full((1, D_MODEL)), full((D_MODEL, LANES)), full((1, LANES)),
        ],
        out_specs=[out_tile(D_MODEL), out_tile(D_MODEL), out_tile(LANES), out_tile(LANES)],
        out_shape=[
            jax.ShapeDtypeStruct((b, s, D_MODEL), F32),
            jax.ShapeDtypeStruct((b, s, D_MODEL), F32),
            jax.ShapeDtypeStruct((b, s, LANES), jnp.int32),
            jax.ShapeDtypeStruct((b, s, LANES), F32),
        ],
        scratch_shapes=[pltpu.VMEM((D_MODEL // LANES, ts + 2 * MIX_HALO, LANES), F32),
                        pltpu.VMEM((ts, D_MODEL), F32)],
        compiler_params=_cparams(("arbitrary", "arbitrary")),
        name="mix",
    )(z3, z3, z3, z3, z3, z3, g3, z3, z3, x3, dw, db, lng, lnb, wcp, bcp, wrp, wo, nfg, wr, br)


MOE_BM = 256
MOE_TT = 256
MOE_PER = TOP_K * MOE_TT
CHUNKS = D_MODEL // LANES


def _to_flat(flat_ref, lead, x):
    rows = x.shape[0]
    for c in range(CHUNKS):
        flat_ref[lead + (pl.ds(c, rows, stride=CHUNKS), slice(None))] = x[:, c * LANES:(c + 1) * LANES]


def _from_flat(flat_ref, lead, rows):
    return jnp.concatenate(
        [flat_ref[lead + (pl.ds(c, rows, stride=CHUNKS), slice(None))] for c in range(CHUNKS)], axis=1)


def _run_copies(cnt_ref, off_ref, glob_ref, tile, make):
    for e in range(N_EXPERTS):
        k = tile * N_EXPERTS + e
        n = cnt_ref[k]

        @pl.when(n > 0)
        def _(k=k, n=n):
            make(pl.multiple_of(off_ref[k] * CHUNKS, CHUNKS), pl.multiple_of(glob_ref[k] * CHUNKS, CHUNKS),
                 pl.multiple_of(n * CHUNKS, CHUNKS)).start()


def _disp_kernel(cnt_ref, off_ref, glob_ref, zst_ref, zn_ref, ord_ref, h_ref, xs_hbm, hflat, cbuf, zbuf, sem, zsem):
    i = pl.program_id(0)
    last = pl.num_programs(0) - 1
    slot = i % 2

    def whole(s):
        return pltpu.make_async_copy(cbuf.at[s], xs_hbm.at[pl.ds(0, MOE_PER * CHUNKS)], sem.at[s])

    @pl.when(i >= 2)
    def _():
        whole(slot).wait()

    _to_flat(hflat, (), h_ref[...])

    def compact(it, carry):
        for u in range(SUBLANES):
            j = it * SUBLANES + u
            src = pl.multiple_of(ord_ref[0, 0, j] * CHUNKS, CHUNKS)
            cbuf[slot, pl.ds(pl.multiple_of(j * CHUNKS, CHUNKS), CHUNKS), :] = hflat[pl.ds(src, CHUNKS), :]
        return carry

    lax.fori_loop(0, MOE_PER // SUBLANES, compact, 0)

    _run_copies(cnt_ref, off_ref, glob_ref, i, lambda loc, glob, n: pltpu.make_async_copy(
        cbuf.at[slot, pl.ds(loc, n)], xs_hbm.at[pl.ds(glob, n)], sem.at[slot]))

    @pl.when(i == last)
    def _():
        zbuf[...] = jnp.zeros(zbuf.shape, zbuf.dtype)
        for e in range(N_EXPERTS):
            n = zn_ref[e]

            @pl.when(n > 0)
            def _(e=e, n=n):
                rows = pl.multiple_of(n * CHUNKS, CHUNKS)
                cp = pltpu.make_async_copy(zbuf.at[pl.ds(0, rows)],
                                           xs_hbm.at[pl.ds(pl.multiple_of(zst_ref[e] * CHUNKS, CHUNKS), rows)], zsem)
                cp.start()
                cp.wait()

        def zero_block(k, carry):
            row0 = pl.multiple_of((zst_ref[N_EXPERTS] + k * MOE_BM) * CHUNKS, CHUNKS)
            cp = pltpu.make_async_copy(zbuf, xs_hbm.at[pl.ds(row0, MOE_BM * CHUNKS)], zsem)
            cp.start()
            cp.wait()
            return carry

        lax.fori_loop(0, zn_ref[N_EXPERTS], zero_block, 0)

        @pl.when(i >= 1)
        def _():
            whole(1 - slot).wait()

        whole(slot).wait()


def _dispatch(h2, tables, ord_local, n_rows):
    t = h2.shape[0]
    nt = t // MOE_TT
    cnt, off, glob, zst, zn = tables
    grid_spec = pltpu.PrefetchScalarGridSpec(
        num_scalar_prefetch=5,
        grid=(nt,),
        in_specs=[
            pl.BlockSpec((1, 1, MOE_PER), lambda i, *_: (i, 0, 0), memory_space=pltpu.SMEM),
            pl.BlockSpec((MOE_TT, D_MODEL), lambda i, *_: (i, 0)),
        ],
        out_specs=pl.BlockSpec(memory_space=pl.ANY),
        scratch_shapes=[
            pltpu.VMEM((MOE_TT * CHUNKS, LANES), F32),
            pltpu.VMEM((2, MOE_PER * CHUNKS, LANES), F32),
            pltpu.VMEM((MOE_BM * CHUNKS, LANES), F32),
            pltpu.SemaphoreType.DMA((2,)),
            pltpu.SemaphoreType.DMA(()),
        ],
    )
    return pl.pallas_call(
        _disp_kernel,
        grid_spec=grid_spec,
        out_shape=jax.ShapeDtypeStruct((n_rows * CHUNKS, LANES), F32),
        compiler_params=_cparams(("arbitrary",)),
        name="dispatch",
    )(cnt, off, glob, zst, zn, ord_local, h2)


def _experts_kernel(be_ref, nused_ref, xs_ref, wgu0_ref, bgu0_ref, wd0_ref, bd0_ref, wgu1_ref, bgu1_ref, wd1_ref,
                    bd1_ref, ys_ref):
    del be_ref
    bm = MOE_BM
    first = 2 * pl.program_id(0)

    @pl.when(first < nused_ref[0])
    def _():
        for j, (wgu_ref, bgu_ref, wd_ref, bd_ref) in enumerate(((wgu0_ref, bgu0_ref, wd0_ref, bd0_ref),
                                                               (wgu1_ref, bgu1_ref, wd1_ref, bd1_ref))):
            rows = xs_ref.at[pl.ds(j * bm * CHUNKS, bm * CHUNKS)]
            xb = _from_flat(rows, (), bm).astype(BF16)
            gu = jnp.dot(xb, wgu_ref[...], preferred_element_type=F32) + bgu_ref[...]
            gt = jnp.minimum(gu[:, :D_EXPERT], SWIGLU_LIMIT)
            upv = jnp.clip(gu[:, D_EXPERT:], -SWIGLU_LIMIT, SWIGLU_LIMIT)
            act = (upv + 1.0) * (gt * _sigmoid(gt * SWIGLU_ALPHA))
            y = jnp.dot(act.astype(BF16), wd_ref[...], preferred_element_type=F32) + bd_ref[...]
            _to_flat(ys_ref.at[pl.ds(j * bm * CHUNKS, bm * CHUNKS)], (), y)

    @pl.when(first >= nused_ref[0])
    def _():
        ys_ref[...] = jnp.zeros(ys_ref.shape, ys_ref.dtype)


def _experts(xs, block_e, n_used, wgu, bgu, wd, bd):
    nb = block_e.shape[0]
    bm = MOE_BM
    wspec = lambda shape, j: pl.BlockSpec((None,) + shape, lambda i, be, nu, j=j: (be[2 * i + j], 0, 0))
    weights = lambda j: [wspec((D_MODEL, 2 * D_EXPERT), j), wspec((1, 2 * D_EXPERT), j),
                         wspec((D_EXPERT, D_MODEL), j), wspec((1, D_MODEL), j)]
    grid_spec = pltpu.PrefetchScalarGridSpec(
        num_scalar_prefetch=2,
        grid=(nb // 2,),
        in_specs=[pl.BlockSpec((2 * bm * CHUNKS, LANES), lambda i, be, nu: (i, 0))] + weights(0) + weights(1),
        out_specs=pl.BlockSpec((2 * bm * CHUNKS, LANES), lambda i, be, nu: (i, 0)),
    )
    return pl.pallas_call(
        _experts_kernel,
        grid_spec=grid_spec,
        out_shape=jax.ShapeDtypeStruct(xs.shape, F32),
        compiler_params=_cparams(("arbitrary",)),
        name="experts",
    )(block_e, n_used, xs, wgu, bgu, wd, bd, wgu, bgu, wd, bd)


def _combine_kernel(cnt_ref, off_ref, glob_ref, x1_ref, tok_ref, gate_ref, p_ref, npg_ref, wpg_ref, wpp_ref, ppg_ref,
                    fg_ref, ys_hbm, o_ref, gbuf, sem):
    i = pl.program_id(0)
    nt = pl.num_programs(0)
    slot = i % 2

    def fetch(tile, s):
        _run_copies(cnt_ref, off_ref, glob_ref, tile, lambda loc, glob, n: pltpu.make_async_copy(
            ys_hbm.at[pl.ds(glob, n)], gbuf.at[s, pl.ds(loc, n)], sem.at[s]))

    @pl.when(i == 0)
    def _():
        fetch(0, 0)

    @pl.when(i + 1 < nt)
    def _():
        fetch(i + 1, 1 - slot)

    pltpu.make_async_copy(ys_hbm.at[pl.ds(0, MOE_PER * CHUNKS)], gbuf.at[slot], sem.at[slot]).wait()

    y = _from_flat(gbuf, (slot,), MOE_PER).astype(BF16)
    tok_of_row = tok_ref[0]
    hit = lax.broadcasted_iota(jnp.int32, (MOE_TT, MOE_PER), 0) == tok_of_row
    sel = jnp.where(hit, gate_ref[0], 0.0).astype(BF16)
    x2 = x1_ref[...] + jnp.dot(sel, y, preferred_element_type=F32)
    h = _rms(x2, npg_ref[...])
    gate = _sigmoid(jnp.dot(h.astype(BF16), wpg_ref[...], preferred_element_type=F32))
    emb = _rms(jnp.dot(p_ref[...].astype(BF16), wpp_ref[...], preferred_element_type=F32), ppg_ref[...])
    x3 = x2 + gate * emb
    o_ref[...] = _rms(x3, fg_ref[...])


def _combine(x1, ys, tables, tok_of_row, gate_of_row, p, npg, wpg, wpp, ppg, fg):
    t = x1.shape[0]
    tm = MOE_TT
    nt = t // tm
    cnt, off, glob = tables
    full = lambda shape: pl.BlockSpec(shape, lambda i, *_: (0,) * len(shape))
    grid_spec = pltpu.PrefetchScalarGridSpec(
        num_scalar_prefetch=3,
        grid=(nt,),
        in_specs=[
            pl.BlockSpec((tm, D_MODEL), lambda i, *_: (i, 0)),
            pl.BlockSpec((1, 1, MOE_PER), lambda i, *_: (i, 0, 0)),
            pl.BlockSpec((1, 1, MOE_PER), lambda i, *_: (i, 0, 0)),
            pl.BlockSpec((tm, D_PLE), lambda i, *_: (i, 0)),
            full((1, D_MODEL)), full((D_MODEL, D_MODEL)), full((D_PLE, D_MODEL)), full((1, D_MODEL)),
            full((1, D_MODEL)),
            pl.BlockSpec(memory_space=pl.ANY),
        ],
        out_specs=pl.BlockSpec((tm, D_MODEL), lambda i, *_: (i, 0)),
        scratch_shapes=[pltpu.VMEM((2, MOE_PER * CHUNKS, LANES), F32), pltpu.SemaphoreType.DMA((2,))],
    )
    return pl.pallas_call(
        _combine_kernel,
        grid_spec=grid_spec,
        out_shape=jax.ShapeDtypeStruct((t, D_MODEL), F32),
        compiler_params=_cparams(("arbitrary",)),
        name="combine",
    )(cnt, off, glob, x1, tok_of_row, gate_of_row, p, npg, wpg, wpp, ppg, fg, ys)


def _route_tiles(ridx, rgate, bm):
    t = ridx.shape[0]
    tk = t * TOP_K
    nt = t // MOE_TT
    flat_e = ridx[:, :TOP_K].reshape(-1)
    flat_g = rgate[:, :TOP_K].reshape(-1)
    pos = jnp.arange(tk, dtype=jnp.int32)
    tile_of = pos // MOE_PER
    order = jnp.argsort(tile_of * N_EXPERTS + flat_e, stable=True).astype(jnp.int32)
    tok_local = (lax.shift_right_logical(order, 2) - tile_of * MOE_TT).astype(jnp.int32)
    gate_of_row = flat_g[order]
    experts = jnp.arange(N_EXPERTS, dtype=jnp.int32)
    cnt = jnp.sum((flat_e.reshape(nt, MOE_PER, 1) == experts[None, None, :]).astype(jnp.int32), axis=1)
    off = jnp.cumsum(cnt, axis=1) - cnt
    counts = jnp.sum(cnt, axis=0)
    nblk = (counts + bm - 1) // bm
    blk_end = jnp.cumsum(nblk)
    pad_start = (blk_end - nblk) * bm
    glob = pad_start[None, :] + jnp.cumsum(cnt, axis=0) - cnt
    n_blocks = tk // bm + N_EXPERTS
    b = jnp.arange(n_blocks, dtype=jnp.int32)
    block_e = jnp.minimum(jnp.sum((b[:, None] >= blk_end[None, :]).astype(jnp.int32), axis=1), N_EXPERTS - 1)
    flat = lambda v: v.reshape(-1).astype(jnp.int32)
    tables = (flat(cnt), flat(off), flat(glob))
    zero_runs = (flat(jnp.concatenate([pad_start + counts, blk_end[-1:] * bm])),
                 flat(jnp.concatenate([nblk * bm - counts, n_blocks - blk_end[-1:]])))
    shape = (nt, 1, MOE_PER)
    return (tables, zero_runs, tok_local.reshape(shape), gate_of_row.reshape(shape), block_e.astype(jnp.int32),
            blk_end[-1:].astype(jnp.int32), n_blocks * bm)


def _block_diag(w):
    per = RNN_CB // RNN_HEAD_DIM
    groups = D_MODEL // RNN_CB
    w = w.reshape(2, groups, per, RNN_HEAD_DIM, RNN_HEAD_DIM)
    eye = jnp.eye(per, dtype=w.dtype)
    bd = jnp.einsum("dgpij,pq->dgpiqj", w, eye)
    return bd.reshape(2, groups, RNN_CB, RNN_CB)


def _trunk(x, p, wts):
    b, s, d = x.shape
    t = b * s
    row = lambda v: v.reshape(1, -1)
    z = _inproj(x.reshape(t, d), row(wts["norm_mix_g"]), wts["w_in"], row(wts["b_in"]))
    z3 = z.reshape(b, s, D_IN)
    g3 = _rnn(z3, wts["conv_rnn_w"], row(wts["conv_rnn_b"]), wts["w_a_bd"], wts["b_a"], wts["w_x_bd"], wts["b_x"],
              wts["rg_c"])
    x1, h2, ridx, rgate = _mix(z3, g3, x, wts["conv_dw_w"], row(wts["conv_dw_b"]), row(wts["ln_g"]),
                               row(wts["ln_b"]), wts["w_conv_proj"], row(wts["b_conv_proj"]), wts["w_rnn_proj"],
                               wts["w_out"], row(wts["norm_ffn_g"]), wts["w_router"], wts["b_router"])
    tables, zero_runs, tok_of_row, gate_of_row, block_e, n_used, n_rows = _route_tiles(
        ridx.reshape(t, LANES), rgate.reshape(t, LANES), MOE_BM)
    xs = _dispatch(h2.reshape(t, d), tables + zero_runs, tok_of_row, n_rows)
    ys = _experts(xs, block_e, n_used, wts["w_gu"], wts["b_gu"], wts["w_down"], wts["b_down"])
    y = _combine(x1.reshape(t, d), ys, tables, tok_of_row, gate_of_row, p.reshape(t, D_PLE),
                 row(wts["norm_ple_g"]), wts["w_ple_gate"], wts["w_ple_proj"], row(wts["ple_post_g"]),
                 row(wts["final_g"]))
    return y.reshape(b, s, d)


def kernel(x_prompt, x_sample, p_prompt, p_sample, norm_mix_g, w_in, b_in, conv_rnn_w, conv_rnn_b, w_a, b_a, w_x,
           b_x, lam, w_rnn_proj, conv_dw_w, conv_dw_b, ln_g, ln_b, w_conv_proj, b_conv_proj, w_out, norm_ffn_g,
           w_router, b_router, w_gu, b_gu, w_down, b_down, norm_ple_g, w_ple_gate, w_ple_proj, ple_post_g, final_g):
    l = 0
    wts = {
        "norm_mix_g": norm_mix_g[l], "w_in": w_in[l].astype(BF16), "b_in": b_in[l],
        "conv_rnn_w": conv_rnn_w[l], "conv_rnn_b": conv_rnn_b[l],
        "w_a_bd": _block_diag(w_a[l]).astype(BF16), "b_a": b_a[l],
        "w_x_bd": _block_diag(w_x[l]).astype(BF16), "b_x": b_x[l],
        "rg_c": RG_C * jax.nn.softplus(-lam[l]),
        "w_rnn_proj": w_rnn_proj[l].astype(BF16),
        "conv_dw_w": conv_dw_w[l], "conv_dw_b": conv_dw_b[l], "ln_g": ln_g[l], "ln_b": ln_b[l],
        "w_conv_proj": w_conv_proj[l].astype(BF16), "b_conv_proj": b_conv_proj[l],
        "w_out": w_out[l].astype(BF16), "norm_ffn_g": norm_ffn_g[l],
        "w_router": jnp.pad(w_router[l], ((0, 0), (0, LANES - N_EXPERTS))).astype(BF16),
        "b_router": jnp.pad(b_router[l], (0, LANES - N_EXPERTS), constant_values=NEG_INF).reshape(1, LANES),
        "w_gu": w_gu[l].astype(BF16), "b_gu": b_gu[l].reshape(N_EXPERTS, 1, 2 * D_EXPERT),
        "w_down": w_down[l].astype(BF16), "b_down": b_down[l].reshape(N_EXPERTS, 1, D_MODEL),
        "norm_ple_g": norm_ple_g[l], "w_ple_gate": w_ple_gate[l].astype(BF16),
        "w_ple_proj": w_ple_proj[l].astype(BF16), "ple_post_g": ple_post_g[l], "final_g": final_g,
    }
    y_prompt = _trunk(x_prompt, p_prompt[l], wts)
    y_sample = _trunk(x_sample, p_sample[l], wts)
    return (y_prompt, y_sample)
```

```python
import jax
import jax.numpy as jnp
from jax import lax
from jax.experimental import pallas as pl
from jax.experimental.pallas import tpu as pltpu

D_MODEL = 1024
D_IN = 6 * D_MODEL
RNN_HEADS = 16
RNN_HEAD_DIM = D_MODEL // RNN_HEADS
RNN_CONV = 4
RG_C = 8.0
CONV_WIDTH = 31
N_EXPERTS = 32
TOP_K = 4
D_EXPERT = D_MODEL
SWIGLU_LIMIT = 7.0
SWIGLU_ALPHA = 1.702
D_PLE = 256
EPS = 1e-6

LANES = 128
SUBLANES = 8
NEG_INF = float("-inf")

F32 = jnp.float32
BF16 = jnp.bfloat16

VMEM_LIMIT = 56 * 1024 * 1024


def _cparams(sem):
    return pltpu.CompilerParams(dimension_semantics=sem, vmem_limit_bytes=VMEM_LIMIT)


def _rms(x, g):
    return x * lax.rsqrt(jnp.mean(x * x, axis=-1, keepdims=True) + EPS) * g


def _sigmoid(x):
    return 0.5 * jnp.tanh(0.5 * x) + 0.5


def _inproj_kernel(x_ref, g_ref, w_ref, b_ref, z_ref, h_scr):
    @pl.when(pl.program_id(1) == 0)
    def _():
        h_scr[...] = _rms(x_ref[...], g_ref[...]).astype(BF16)

    z_ref[...] = jnp.dot(h_scr[...], w_ref[...], preferred_element_type=F32) + b_ref[...]


def _inproj(x, g, w, b, tm=1024, tn=2048):
    t = x.shape[0]
    return pl.pallas_call(
        _inproj_kernel,
        grid=(t // tm, D_IN // tn),
        in_specs=[
            pl.BlockSpec((tm, D_MODEL), lambda i, j: (i, 0)),
            pl.BlockSpec((1, D_MODEL), lambda i, j: (0, 0)),
            pl.BlockSpec((D_MODEL, tn), lambda i, j: (0, j)),
            pl.BlockSpec((1, tn), lambda i, j: (0, j)),
        ],
        out_specs=pl.BlockSpec((tm, tn), lambda i, j: (i, j)),
        out_shape=jax.ShapeDtypeStruct((t, D_IN), F32),
        scratch_shapes=[pltpu.VMEM((tm, D_MODEL), BF16)],
        compiler_params=_cparams(("arbitrary", "arbitrary")),
        name="inproj",
    )(x, g, w, b)


RNN_CB = 256
RNN_CHUNK = 128


def _rnn_kernel(xr_ref, yr_ref, cw_ref, cb_ref, wa_ref, ba_ref, wx_ref, bx_ref, c_ref, o_ref, xpad, hf):
    s, cb = xr_ref.shape
    chunk = RNN_CHUNK
    nchunks = s // chunk
    zeros8 = jnp.zeros((SUBLANES, LANES), F32)
    for c in range(cb // LANES):
        xpad[c, 0:SUBLANES, :] = zeros8
        xpad[c, s + SUBLANES:s + 2 * SUBLANES, :] = zeros8
        xpad[c, SUBLANES:s + SUBLANES, :] = xr_ref[:, c * LANES:(c + 1) * LANES]
    row8 = lax.broadcasted_iota(jnp.int32, (SUBLANES, cb), 0)

    def conv(t0):
        parts = []
        for c in range(cb // LANES):
            lanes = slice(c * LANES, (c + 1) * LANES)
            acc = cb_ref[:, lanes] + cw_ref[0:1, lanes] * xpad[c, pl.ds(t0 + 6, chunk), :]
            for k in range(1, RNN_CONV):
                acc = acc + cw_ref[k:k + 1, lanes] * xpad[c, pl.ds(t0 + (6 + k), chunk), :]
            parts.append(acc)
        return jnp.concatenate(parts, axis=1)

    def gates(xc, d):
        xb = xc.astype(BF16)
        ta = jnp.tanh(jnp.dot(xb, wa_ref[d], preferred_element_type=F32) + ba_ref[d:d + 1, :])
        ti = jnp.tanh(jnp.dot(xb, wx_ref[d], preferred_element_type=F32) + bx_ref[d:d + 1, :])
        ch = c_ref[d:d + 1, :]
        a = jnp.exp(ch * ta + ch)
        u = jnp.sqrt(1.0 - a * a) * ((0.5 * ti + 0.5) * xc)
        return a, u

    def tile_scan(a, u, reverse):
        for dist in (1, 2, 4):
            if reverse:
                keep = row8 < SUBLANES - dist
                shift = SUBLANES - dist
            else:
                keep = row8 >= dist
                shift = dist
            us = jnp.where(keep, pltpu.roll(u, shift, 0), 0.0)
            a_s = jnp.where(keep, pltpu.roll(a, shift, 0), 1.0)
            u = u + a * us
            a = a * a_s
        return a, u

    def fwd_body(i, h):
        t0 = pl.multiple_of(i * chunk, chunk)
        a, u = gates(conv(t0), 0)
        tiles = []
        for j in range(chunk // SUBLANES):
            at, ut = tile_scan(a[j * 8:(j + 1) * 8], u[j * 8:(j + 1) * 8], False)
            ht = ut + at * h
            h = ht[7:8, :]
            tiles.append(ht)
        hf[pl.ds(t0, chunk), :] = jnp.concatenate(tiles, axis=0)
        return h

    lax.fori_loop(0, nchunks, fwd_body, jnp.zeros((1, cb), F32), unroll=4)

    def bwd_body(i, h):
        t0 = pl.multiple_of((nchunks - 1 - i) * chunk, chunk)
        a, u = gates(conv(t0), 1)
        tiles = [None] * (chunk // SUBLANES)
        for j in reversed(range(chunk // SUBLANES)):
            at, ut = tile_scan(a[j * 8:(j + 1) * 8], u[j * 8:(j + 1) * 8], True)
            ht = ut + at * h
            h = ht[0:1, :]
            tiles[j] = ht
        hb = jnp.concatenate(tiles, axis=0)
        y = yr_ref[pl.ds(t0, chunk), :]
        o_ref[pl.ds(t0, chunk), :] = ((hf[pl.ds(t0, chunk), :] + hb) * jax.nn.gelu(y)).astype(o_ref.dtype)
        return h

    lax.fori_loop(0, nchunks, bwd_body, jnp.zeros((1, cb), F32), unroll=4)


def _rnn(z3, cw, cb_, wa_bd, ba, wx_bd, bx, c):
    b, s, _ = z3.shape
    nc = D_MODEL // RNN_CB
    return pl.pallas_call(
        _rnn_kernel,
        grid=(b, nc),
        in_specs=[
            pl.BlockSpec((None, s, RNN_CB), lambda i, j: (i, 0, j)),
            pl.BlockSpec((None, s, RNN_CB), lambda i, j: (i, 0, nc + j)),
            pl.BlockSpec((RNN_CONV, RNN_CB), lambda i, j: (0, j)),
            pl.BlockSpec((1, RNN_CB), lambda i, j: (0, j)),
            pl.BlockSpec((2, None, RNN_CB, RNN_CB), lambda i, j: (0, j, 0, 0)),
            pl.BlockSpec((2, RNN_CB), lambda i, j: (0, j)),
            pl.BlockSpec((2, None, RNN_CB, RNN_CB), lambda i, j: (0, j, 0, 0)),
            pl.BlockSpec((2, RNN_CB), lambda i, j: (0, j)),
            pl.BlockSpec((2, RNN_CB), lambda i, j: (0, j)),
        ],
        out_specs=pl.BlockSpec((None, s, RNN_CB), lambda i, j: (i, 0, j)),
        out_shape=jax.ShapeDtypeStruct((b, s, D_MODEL), BF16),
        scratch_shapes=[pltpu.VMEM((RNN_CB // LANES, s + 2 * SUBLANES, LANES), F32),
                        pltpu.VMEM((s, RNN_CB), F32)],
        compiler_params=_cparams(("arbitrary", "arbitrary")),
        name="rnn",
    )(z3, z3, cw, cb_, wa_bd, ba, wx_bd, bx, c)


MIX_TS = 512
MIX_HALO = 16
MIX_RC = 64


def _mix_kernel(cv_ref, cg_ref, cvp_ref, cgp_ref, cvn_ref, cgn_ref, g_ref, gr_ref, gc_ref, x_ref,
                dw_ref, db_ref, lng_ref, lnb_ref, wcp_ref, bcp_ref, wrp_ref, wo_ref, nfg_ref, wr_ref, br_ref,
                x1_ref, h2_ref, ridx_ref, rgate_ref, uext, cscr):
    ts = cv_ref.shape[0]
    i = pl.program_id(1)
    last = pl.num_programs(1) - 1
    up = jnp.where(i > 0, cvp_ref[...] * _sigmoid(cgp_ref[...]), 0.0)
    un = jnp.where(i < last, cvn_ref[...] * _sigmoid(cgn_ref[...]), 0.0)
    um = cv_ref[...] * _sigmoid(cg_ref[...])
    for c in range(D_MODEL // LANES):
        lanes = slice(c * LANES, (c + 1) * LANES)
        uext[c, 0:MIX_HALO, :] = up[:, lanes]
        uext[c, MIX_HALO:MIX_HALO + ts, :] = um[:, lanes]
        uext[c, MIX_HALO + ts:2 * MIX_HALO + ts, :] = un[:, lanes]

    off = MIX_HALO - CONV_WIDTH // 2

    rnn = jnp.dot(g_ref[...], wrp_ref[...], preferred_element_type=F32)

    for rc in range(ts // MIX_RC):
        r0 = rc * MIX_RC
        for c in range(D_MODEL // LANES):
            lanes = slice(c * LANES, (c + 1) * LANES)
            acc = db_ref[:, lanes] + dw_ref[0:1, lanes] * uext[c, pl.ds(r0 + off, MIX_RC), :]
            for k in range(1, CONV_WIDTH):
                acc = acc + dw_ref[k:k + 1, lanes] * uext[c, pl.ds(r0 + (off + k), MIX_RC), :]
            cscr[pl.ds(r0, MIX_RC), lanes] = acc

    c = cscr[...]
    mu = jnp.mean(c, axis=-1, keepdims=True)
    cc = c - mu
    var = jnp.mean(cc * cc, axis=-1, keepdims=True)
    y = cc * lax.rsqrt(var + EPS) * lng_ref[...] + lnb_ref[...]
    sw = y * _sigmoid(y)
    conv = jnp.dot(sw.astype(BF16), wcp_ref[...], preferred_element_type=F32) + bcp_ref[...]
    merged = _sigmoid(gr_ref[...]) * rnn + _sigmoid(gc_ref[...]) * conv
    x1 = x_ref[...] + jnp.dot(merged.astype(BF16), wo_ref[...], preferred_element_type=F32)
    x1_ref[...] = x1
    h2 = _rms(x1, nfg_ref[...])
    h2_ref[...] = h2

    logits = jnp.dot(h2.astype(BF16), wr_ref[...], preferred_element_type=F32) + br_ref[...]
    lane = lax.broadcasted_iota(jnp.int32, logits.shape, 1)
    vals, idxs = [], []
    for _ in range(TOP_K):
        m = jnp.max(logits, axis=-1, keepdims=True)
        idx = jnp.min(jnp.where(logits == m, lane, LANES), axis=-1, keepdims=True)
        vals.append(m)
        idxs.append(idx)
        logits = jnp.where(lane == idx, NEG_INF, logits)
    es = [jnp.exp(v - vals[0]) for v in vals]
    inv = 1.0 / (es[0] + es[1] + es[2] + es[3])
    ridx = jnp.zeros(logits.shape, jnp.int32)
    rgate = jnp.zeros(logits.shape, F32)
    for k in range(TOP_K):
        ridx = jnp.where(lane == k, idxs[k], ridx)
        rgate = jnp.where(lane == k, es[k] * inv, rgate)
    ridx_ref[...] = ridx
    rgate_ref[...] = rgate


def _mix(z3, g3, x3, dw, db, lng, lnb, wcp, bcp, wrp, wo, nfg, wr, br):
    b, s, _ = z3.shape
    ts = MIX_TS
    nt = s // ts
    hb = ts // MIX_HALO
    nhb = s // MIX_HALO
    tile = lambda col: pl.BlockSpec((None, ts, D_MODEL), lambda i, j, col=col: (i, j, col))
    prev = lambda col: pl.BlockSpec((None, MIX_HALO, D_MODEL),
                                    lambda i, j, col=col: (i, jnp.maximum(j * hb - 1, 0), col))
    nxt = lambda col: pl.BlockSpec((None, MIX_HALO, D_MODEL),
                                   lambda i, j, col=col: (i, jnp.minimum((j + 1) * hb, nhb - 1), col))
    full = lambda shape: pl.BlockSpec(shape, lambda i, j: (0,) * len(shape))
    out_tile = lambda w: pl.BlockSpec((None, ts, w), lambda i, j: (i, j, 0))
    return pl.pallas_call(
        _mix_kernel,
        grid=(b, nt),
        in_specs=[
            tile(2), tile(3), prev(2), prev(3), nxt(2), nxt(3),
            tile(0),
            tile(4), tile(5),
            tile(0),
            full((CONV_WIDTH, D_MODEL)), full((1, D_MODEL)), full((1, D_MODEL)), full((1, D_MODEL)),
            full((D_MODEL, D_MODEL)), full((1, D_MODEL)), full((D_MODEL, D_MODEL)), full((D_MODEL, D_MODEL)),
            full((1, D_MODEL)), full((D_MODEL, LANES)), full((1, LANES)),
        ],
        out_specs=[out_tile(D_MODEL), out_tile(D_MODEL), out_tile(LANES), out_tile(LANES)],
        out_shape=[
            jax.ShapeDtypeStruct((b, s, D_MODEL), F32),
            jax.ShapeDtypeStruct((b, s, D_MODEL), F32),
            jax.ShapeDtypeStruct((b, s, LANES), jnp.int32),
            jax.ShapeDtypeStruct((b, s, LANES), F32),
        ],
        scratch_shapes=[pltpu.VMEM((D_MODEL // LANES, ts + 2 * MIX_HALO, LANES), F32),
                        pltpu.VMEM((ts, D_MODEL), F32)],
        compiler_params=_cparams(("arbitrary", "arbitrary")),
        name="mix",
    )(z3, z3, z3, z3, z3, z3, g3, z3, z3, x3, dw, db, lng, lnb, wcp, bcp, wrp, wo, nfg, wr, br)


MOE_BM = 256
MOE_TT = 256
MOE_PER = TOP_K * MOE_TT
CHUNKS = D_MODEL // LANES


def _to_flat(flat_ref, lead, x):
    rows = x.shape[0]
    for c in range(CHUNKS):
        flat_ref[lead + (pl.ds(c, rows, stride=CHUNKS), slice(None))] = x[:, c * LANES:(c + 1) * LANES]


def _from_flat(flat_ref, lead, rows):
    return jnp.concatenate(
        [flat_ref[lead + (pl.ds(c, rows, stride=CHUNKS), slice(None))] for c in range(CHUNKS)], axis=1)


def _run_copies(cnt_ref, off_ref, glob_ref, tile, make):
    for e in range(N_EXPERTS):
        k = tile * N_EXPERTS + e
        n = cnt_ref[k]

        @pl.when(n > 0)
        def _(k=k, n=n):
            make(pl.multiple_of(off_ref[k] * CHUNKS, CHUNKS), pl.multiple_of(glob_ref[k] * CHUNKS, CHUNKS),
                 pl.multiple_of(n * CHUNKS, CHUNKS)).start()


def _disp_kernel(cnt_ref, off_ref, glob_ref, zst_ref, zn_ref, ord_ref, h_ref, xs_hbm, hflat, cbuf, zbuf, sem, zsem):
    i = pl.program_id(0)
    last = pl.num_programs(0) - 1
    slot = i % 2

    def whole(s):
        return pltpu.make_async_copy(cbuf.at[s], xs_hbm.at[pl.ds(0, MOE_PER * CHUNKS)], sem.at[s])

    @pl.when(i >= 2)
    def _():
        whole(slot).wait()

    _to_flat(hflat, (), h_ref[...])

    group = 2 * SUBLANES
    out_rows = cbuf.at[slot]

    def compact(it, carry):
        j0 = it * group
        dst = out_rows.at[pl.ds(pl.multiple_of(j0 * CHUNKS, group * CHUNKS), group * CHUNKS)]
        for u in range(group):
            src = pl.multiple_of(ord_ref[0, 0, j0 + u] * CHUNKS, CHUNKS)
            dst[u * CHUNKS:(u + 1) * CHUNKS, :] = hflat[pl.ds(src, CHUNKS), :]
        return carry

    lax.fori_loop(0, MOE_PER // group, compact, 0)

    _run_copies(cnt_ref, off_ref, glob_ref, i, lambda loc, glob, n: pltpu.make_async_copy(
        cbuf.at[slot, pl.ds(loc, n)], xs_hbm.at[pl.ds(glob, n)], sem.at[slot]))

    @pl.when(i == last)
    def _():
        zbuf[...] = jnp.zeros(zbuf.shape, zbuf.dtype)
        for e in range(N_EXPERTS):
            n = zn_ref[e]

            @pl.when(n > 0)
            def _(e=e, n=n):
                rows = pl.multiple_of(n * CHUNKS, CHUNKS)
                cp = pltpu.make_async_copy(zbuf.at[pl.ds(0, rows)],
                                           xs_hbm.at[pl.ds(pl.multiple_of(zst_ref[e] * CHUNKS, CHUNKS), rows)], zsem)
                cp.start()
                cp.wait()

        def zero_block(k, carry):
            row0 = pl.multiple_of((zst_ref[N_EXPERTS] + k * MOE_BM) * CHUNKS, CHUNKS)
            cp = pltpu.make_async_copy(zbuf, xs_hbm.at[pl.ds(row0, MOE_BM * CHUNKS)], zsem)
            cp.start()
            cp.wait()
            return carry

        lax.fori_loop(0, zn_ref[N_EXPERTS], zero_block, 0)

        @pl.when(i >= 1)
        def _():
            whole(1 - slot).wait()

        whole(slot).wait()


def _dispatch(h2, tables, ord_local, n_rows):
    t = h2.shape[0]
    nt = t // MOE_TT
    cnt, off, glob, zst, zn = tables
    grid_spec = pltpu.PrefetchScalarGridSpec(
        num_scalar_prefetch=5,
        grid=(nt,),
        in_specs=[
            pl.BlockSpec((1, 1, MOE_PER), lambda i, *_: (i, 0, 0), memory_space=pltpu.SMEM),
            pl.BlockSpec((MOE_TT, D_MODEL), lambda i, *_: (i, 0)),
        ],
        out_specs=pl.BlockSpec(memory_space=pl.ANY),
        scratch_shapes=[
            pltpu.VMEM((MOE_TT * CHUNKS, LANES), F32),
            pltpu.VMEM((2, MOE_PER * CHUNKS, LANES), F32),
            pltpu.VMEM((MOE_BM * CHUNKS, LANES), F32),
            pltpu.SemaphoreType.DMA((2,)),
            pltpu.SemaphoreType.DMA(()),
        ],
    )
    return pl.pallas_call(
        _disp_kernel,
        grid_spec=grid_spec,
        out_shape=jax.ShapeDtypeStruct((n_rows * CHUNKS, LANES), F32),
        compiler_params=_cparams(("arbitrary",)),
        name="dispatch",
    )(cnt, off, glob, zst, zn, ord_local, h2)


def _experts_kernel(be_ref, nused_ref, xs_ref, wgu0_ref, bgu0_ref, wd0_ref, bd0_ref, wgu1_ref, bgu1_ref, wd1_ref,
                    bd1_ref, ys_ref):
    del be_ref
    bm = MOE_BM
    first = 2 * pl.program_id(0)

    @pl.when(first < nused_ref[0])
    def _():
        for j, (wgu_ref, bgu_ref, wd_ref, bd_ref) in enumerate(((wgu0_ref, bgu0_ref, wd0_ref, bd0_ref),
                                                               (wgu1_ref, bgu1_ref, wd1_ref, bd1_ref))):
            rows = xs_ref.at[pl.ds(j * bm * CHUNKS, bm * CHUNKS)]
            xb = _from_flat(rows, (), bm).astype(BF16)
            gu = jnp.dot(xb, wgu_ref[...], preferred_element_type=F32) + bgu_ref[...]
            gt = jnp.minimum(gu[:, :D_EXPERT], SWIGLU_LIMIT)
            upv = jnp.clip(gu[:, D_EXPERT:], -SWIGLU_LIMIT, SWIGLU_LIMIT)
            act = (upv + 1.0) * (gt * _sigmoid(gt * SWIGLU_ALPHA))
            y = jnp.dot(act.astype(BF16), wd_ref[...], preferred_element_type=F32) + bd_ref[...]
            _to_flat(ys_ref.at[pl.ds(j * bm * CHUNKS, bm * CHUNKS)], (), y)

    @pl.when(first >= nused_ref[0])
    def _():
        ys_ref[...] = jnp.zeros(ys_ref.shape, ys_ref.dtype)


def _experts(xs, block_e, n_used, wgu, bgu, wd, bd):
    nb = block_e.shape[0]
    bm = MOE_BM
    wspec = lambda shape, j: pl.BlockSpec((None,) + shape, lambda i, be, nu, j=j: (be[2 * i + j], 0, 0))
    weights = lambda j: [wspec((D_MODEL, 2 * D_EXPERT), j), wspec((1, 2 * D_EXPERT), j),
                         wspec((D_EXPERT, D_MODEL), j), wspec((1, D_MODEL), j)]
    grid_spec = pltpu.PrefetchScalarGridSpec(
        num_scalar_prefetch=2,
        grid=(nb // 2,),
        in_specs=[pl.BlockSpec((2 * bm * CHUNKS, LANES), lambda i, be, nu: (i, 0))] + weights(0) + weights(1),
        out_specs=pl.BlockSpec((2 * bm * CHUNKS, LANES), lambda i, be, nu: (i, 0)),
    )
    return pl.pallas_call(
        _experts_kernel,
        grid_spec=grid_spec,
        out_shape=jax.ShapeDtypeStruct(xs.shape, F32),
        compiler_params=_cparams(("arbitrary",)),
        name="experts",
    )(block_e, n_used, xs, wgu, bgu, wd, bd, wgu, bgu, wd, bd)


def _combine_kernel(cnt_ref, off_ref, glob_ref, x1_ref, tok_ref, gate_ref, p_ref, npg_ref, wpg_ref, wpp_ref, ppg_ref,
                    fg_ref, ys_hbm, o_ref, gbuf, sem):
    i = pl.program_id(0)
    nt = pl.num_programs(0)
    slot = i % 2

    def fetch(tile, s):
        _run_copies(cnt_ref, off_ref, glob_ref, tile, lambda loc, glob, n: pltpu.make_async_copy(
            ys_hbm.at[pl.ds(glob, n)], gbuf.at[s, pl.ds(loc, n)], sem.at[s]))

    @pl.when(i == 0)
    def _():
        fetch(0, 0)

    @pl.when(i + 1 < nt)
    def _():
        fetch(i + 1, 1 - slot)

    pltpu.make_async_copy(ys_hbm.at[pl.ds(0, MOE_PER * CHUNKS)], gbuf.at[slot], sem.at[slot]).wait()

    y = _from_flat(gbuf, (slot,), MOE_PER).astype(BF16)
    tok_of_row = tok_ref[0]
    hit = lax.broadcasted_iota(jnp.int32, (MOE_TT, MOE_PER), 0) == tok_of_row
    sel = jnp.where(hit, gate_ref[0], 0.0).astype(BF16)
    x2 = x1_ref[...] + jnp.dot(sel, y, preferred_element_type=F32)
    h = _rms(x2, npg_ref[...])
    gate = _sigmoid(jnp.dot(h.astype(BF16), wpg_ref[...], preferred_element_type=F32))
    emb = _rms(jnp.dot(p_ref[...].astype(BF16), wpp_ref[...], preferred_element_type=F32), ppg_ref[...])
    x3 = x2 + gate * emb
    o_ref[...] = _rms(x3, fg_ref[...])


def _combine(x1, ys, tables, tok_of_row, gate_of_row, p, npg, wpg, wpp, ppg, fg):
    t = x1.shape[0]
    tm = MOE_TT
    nt = t // tm
    cnt, off, glob = tables
    full = lambda shape: pl.BlockSpec(shape, lambda i, *_: (0,) * len(shape))
    grid_spec = pltpu.PrefetchScalarGridSpec(
        num_scalar_prefetch=3,
        grid=(nt,),
        in_specs=[
            pl.BlockSpec((tm, D_MODEL), lambda i, *_: (i, 0)),
            pl.BlockSpec((1, 1, MOE_PER), lambda i, *_: (i, 0, 0)),
            pl.BlockSpec((1, 1, MOE_PER), lambda i, *_: (i, 0, 0)),
            pl.BlockSpec((tm, D_PLE), lambda i, *_: (i, 0)),
            full((1, D_MODEL)), full((D_MODEL, D_MODEL)), full((D_PLE, D_MODEL)), full((1, D_MODEL)),
            full((1, D_MODEL)),
            pl.BlockSpec(memory_space=pl.ANY),
        ],
        out_specs=pl.BlockSpec((tm, D_MODEL), lambda i, *_: (i, 0)),
        scratch_shapes=[pltpu.VMEM((2, MOE_PER * CHUNKS, LANES), F32), pltpu.SemaphoreType.DMA((2,))],
    )
    return pl.pallas_call(
        _combine_kernel,
        grid_spec=grid_spec,
        out_shape=jax.ShapeDtypeStruct((t, D_MODEL), F32),
        compiler_params=_cparams(("arbitrary",)),
        name="combine",
    )(cnt, off, glob, x1, tok_of_row, gate_of_row, p, npg, wpg, wpp, ppg, fg, ys)


def _route_tiles(ridx, rgate, bm):
    t = ridx.shape[0]
    tk = t * TOP_K
    nt = t // MOE_TT
    flat_e = ridx[:, :TOP_K].reshape(-1)
    flat_g = rgate[:, :TOP_K].reshape(-1)
    pos = jnp.arange(tk, dtype=jnp.int32)
    tile_of = pos // MOE_PER
    order = jnp.argsort(tile_of * N_EXPERTS + flat_e, stable=True).astype(jnp.int32)
    tok_local = (lax.shift_right_logical(order, 2) - tile_of * MOE_TT).astype(jnp.int32)
    gate_of_row = flat_g[order]
    experts = jnp.arange(N_EXPERTS, dtype=jnp.int32)
    cnt = jnp.sum((flat_e.reshape(nt, MOE_PER, 1) == experts[None, None, :]).astype(jnp.int32), axis=1)
    off = jnp.cumsum(cnt, axis=1) - cnt
    counts = jnp.sum(cnt, axis=0)
    nblk = (counts + bm - 1) // bm
    blk_end = jnp.cumsum(nblk)
    pad_start = (blk_end - nblk) * bm
    glob = pad_start[None, :] + jnp.cumsum(cnt, axis=0) - cnt
    n_blocks = tk // bm + N_EXPERTS
    b = jnp.arange(n_blocks, dtype=jnp.int32)
    block_e = jnp.minimum(jnp.sum((b[:, None] >= blk_end[None, :]).astype(jnp.int32), axis=1), N_EXPERTS - 1)
    flat = lambda v: v.reshape(-1).astype(jnp.int32)
    tables = (flat(cnt), flat(off), flat(glob))
    zero_runs = (flat(jnp.concatenate([pad_start + counts, blk_end[-1:] * bm])),
                 flat(jnp.concatenate([nblk * bm - counts, n_blocks - blk_end[-1:]])))
    shape = (nt, 1, MOE_PER)
    return (tables, zero_runs, tok_local.reshape(shape), gate_of_row.reshape(shape), block_e.astype(jnp.int32),
            blk_end[-1:].astype(jnp.int32), n_blocks * bm)


def _block_diag(w):
    per = RNN_CB // RNN_HEAD_DIM
    groups = D_MODEL // RNN_CB
    w = w.reshape(2, groups, per, RNN_HEAD_DIM, RNN_HEAD_DIM)
    eye = jnp.eye(per, dtype=w.dtype)
    bd = jnp.einsum("dgpij,pq->dgpiqj", w, eye)
    return bd.reshape(2, groups, RNN_CB, RNN_CB)


def _trunk(x, p, wts):
    b, s, d = x.shape
    t = b * s
    row = lambda v: v.reshape(1, -1)
    z = _inproj(x.reshape(t, d), row(wts["norm_mix_g"]), wts["w_in"], row(wts["b_in"]))
    z3 = z.reshape(b, s, D_IN)
    g3 = _rnn(z3, wts["conv_rnn_w"], row(wts["conv_rnn_b"]), wts["w_a_bd"], wts["b_a"], wts["w_x_bd"], wts["b_x"],
              wts["rg_c"])
    x1, h2, ridx, rgate = _mix(z3, g3, x, wts["conv_dw_w"], row(wts["conv_dw_b"]), row(wts["ln_g"]),
                               row(wts["ln_b"]), wts["w_conv_proj"], row(wts["b_conv_proj"]), wts["w_rnn_proj"],
                               wts["w_out"], row(wts["norm_ffn_g"]), wts["w_router"], wts["b_router"])
    tables, zero_runs, tok_of_row, gate_of_row, block_e, n_used, n_rows = _route_tiles(
        ridx.reshape(t, LANES), rgate.reshape(t, LANES), MOE_BM)
    xs = _dispatch(h2.reshape(t, d), tables + zero_runs, tok_of_row, n_rows)
    ys = _experts(xs, block_e, n_used, wts["w_gu"], wts["b_gu"], wts["w_down"], wts["b_down"])
    y = _combine(x1.reshape(t, d), ys, tables, tok_of_row, gate_of_row, p.reshape(t, D_PLE),
                 row(wts["norm_ple_g"]), wts["w_ple_gate"], wts["w_ple_proj"], row(wts["ple_post_g"]),
                 row(wts["final_g"]))
    return y.reshape(b, s, d)


def kernel(x_prompt, x_sample, p_prompt, p_sample, norm_mix_g, w_in, b_in, conv_rnn_w, conv_rnn_b, w_a, b_a, w_x,
           b_x, lam, w_rnn_proj, conv_dw_w, conv_dw_b, ln_g, ln_b, w_conv_proj, b_conv_proj, w_out, norm_ffn_g,
           w_router, b_router, w_gu, b_gu, w_down, b_down, norm_ple_g, w_ple_gate, w_ple_proj, ple_post_g, final_g):
    l = 0
    wts = {
        "norm_mix_g": norm_mix_g[l], "w_in": w_in[l].astype(BF16), "b_in": b_in[l],
        "conv_rnn_w": conv_rnn_w[l], "conv_rnn_b": conv_rnn_b[l],
        "w_a_bd": (0.5 * _block_diag(w_a[l])).astype(BF16), "b_a": 0.5 * b_a[l],
        "w_x_bd": (0.5 * _block_diag(w_x[l])).astype(BF16), "b_x": 0.5 * b_x[l],
        "rg_c": -0.5 * RG_C * jax.nn.softplus(-lam[l]),
        "w_rnn_proj": w_rnn_proj[l].astype(BF16),
        "conv_dw_w": conv_dw_w[l], "conv_dw_b": conv_dw_b[l], "ln_g": ln_g[l], "ln_b": ln_b[l],
        "w_conv_proj": w_conv_proj[l].astype(BF16), "b_conv_proj": b_conv_proj[l],
        "w_out": w_out[l].astype(BF16), "norm_ffn_g": norm_ffn_g[l],
        "w_router": jnp.pad(w_router[l], ((0, 0), (0, LANES - N_EXPERTS))).astype(BF16),
        "b_router": jnp.pad(b_router[l], (0, LANES - N_EXPERTS), constant_values=NEG_INF).reshape(1, LANES),
        "w_gu": w_gu[l].astype(BF16), "b_gu": b_gu[l].reshape(N_EXPERTS, 1, 2 * D_EXPERT),
        "w_down": w_down[l].astype(BF16), "b_down": b_down[l].reshape(N_EXPERTS, 1, D_MODEL),
        "norm_ple_g": norm_ple_g[l], "w_ple_gate": w_ple_gate[l].astype(BF16),
        "w_ple_proj": w_ple_proj[l].astype(BF16), "ple_post_g": ple_post_g[l], "final_g": final_g,
    }
    y_prompt = _trunk(x_prompt, p_prompt[l], wts)
    y_sample = _trunk(x_sample, p_sample[l], wts)
    return (y_prompt, y_sample)
```
